```python
import jax, jax.numpy as jnp
from jax import lax
import numpy as np

D_MODEL = 2048
BATCH = 4
SEQ = 4096
DEPTH = 4

N_MIXERS = 3
HEAD_DIM = 128
GDN_QK_HEADS = D_MODEL // HEAD_DIM
GDN_V_HEADS = 2 * GDN_QK_HEADS
GDN_QK_DIM = GDN_QK_HEADS * HEAD_DIM
GDN_V_DIM = GDN_V_HEADS * HEAD_DIM
GDN_CONV_DIM = 2 * GDN_QK_DIM + GDN_V_DIM
GDN_IN_DIM = GDN_CONV_DIM + GDN_V_DIM + 2 * GDN_V_HEADS
GDN_CONV = 4
GDN_CHUNK = 64
SGU_CHUNK = 128
SGU_INNER = 2 * D_MODEL
SGU_GROUPS = 16
SGU_GROUP_DIM = SGU_INNER // SGU_GROUPS
CONV_WIDTH = 31
FFN_DIM = 7 * D_MODEL // 2
N_EXPERTS = 8
TOP_K = 2
EXPERT_DIM = 7 * D_MODEL // 2
N_GDN = (DEPTH + 2) // 3
N_SGU = (DEPTH + 1) // 3
N_CONV = DEPTH // 3
N_DENSE = (DEPTH + 1) // 2
N_MOE = DEPTH // 2
EPS = 1e-6

kernel_name = 'hybrid_gdn_sgu_conformer_moe'


def rmsnorm(x, g):
    xf = x.astype(jnp.float32)
    y = xf * lax.rsqrt(jnp.mean(xf * xf, axis=-1, keepdims=True) + EPS) * g.astype(jnp.float32)
    return y.astype(x.dtype)


def layernorm(x, g, b):
    xf = x.astype(jnp.float32)
    mu = jnp.mean(xf, axis=-1, keepdims=True)
    xc = xf - mu
    y = xc * lax.rsqrt(jnp.mean(xc * xc, axis=-1, keepdims=True) + 1e-5)
    return (y * g.astype(jnp.float32) + b.astype(jnp.float32)).astype(x.dtype)


def l2norm(x):
    return x * lax.rsqrt(jnp.sum(x * x, axis=-1, keepdims=True) + EPS)


def causal_depthwise_conv(x, w):
    K, C = w.shape
    return lax.conv_general_dilated(
        x, w[:, None, :].astype(x.dtype), window_strides=(1,), padding=[(K - 1, 0)],
        dimension_numbers=('NWC', 'WIO', 'NWC'), feature_group_count=C)


def gated_delta_rule(q, k, v, g, beta):
    Bb, H, S, dk = q.shape
    dv = v.shape[-1]
    n = S // GDN_CHUNK
    C = GDN_CHUNK
    q = q * (dk ** -0.5)
    q, k, v = (t.reshape(Bb, H, n, C, t.shape[-1]) for t in (q, k, v))
    g = jnp.cumsum(g.reshape(Bb, H, n, C), axis=-1)
    beta = beta.reshape(Bb, H, n, C)
    incl = jnp.tril(jnp.ones((C, C), dtype=bool))
    strict = jnp.tril(jnp.ones((C, C), dtype=bool), -1)
    decay = jnp.exp(jnp.where(incl, g[..., :, None] - g[..., None, :], -jnp.inf))
    k_beta = k * beta[..., None]
    v_beta = v * beta[..., None]
    L = jnp.where(strict, jnp.einsum('bhnid,bhnjd->bhnij', k_beta, k) * decay, 0.0)
    eye = jnp.eye(C, dtype=jnp.float32)
    T = lax.linalg.triangular_solve(eye + L, jnp.broadcast_to(eye, L.shape),
                                    left_side=True, lower=True, unit_diagonal=True)
    u = jnp.einsum('bhnij,bhnje->bhnie', T, v_beta)
    w = jnp.einsum('bhnij,bhnjd->bhnid', T, k_beta * jnp.exp(g)[..., None])
    attn = jnp.where(incl, jnp.einsum('bhnid,bhnjd->bhnij', q, k) * decay, 0.0)
    q_g = q * jnp.exp(g)[..., None]
    k_tail = k * jnp.exp(g[..., -1:] - g)[..., None]
    g_last = jnp.exp(g[..., -1])

    def step(state, inp):
        w_c, u_c, qg_c, kt_c, at_c, gl_c = inp
        v_new = u_c - jnp.einsum('bhcd,bhde->bhce', w_c, state)
        o = jnp.einsum('bhcd,bhde->bhce', qg_c, state) + jnp.einsum('bhij,bhje->bhie', at_c, v_new)
        state = state * gl_c[..., None, None] + jnp.einsum('bhcd,bhce->bhde', kt_c, v_new)
        return state, o

    xs = tuple(jnp.moveaxis(t, 2, 0) for t in (w, u, q_g, k_tail, attn, g_last))
    state0 = jnp.zeros((Bb, H, dk, dv), jnp.float32)
    _, o = lax.scan(step, state0, xs)
    return jnp.moveaxis(o, 0, 2).reshape(Bb, H, S, dv)


def gated_deltanet(h, w_in, conv_w, a_log, dt_bias, norm_w, w_out):
    Bb, S, _ = h.shape
    proj = h @ w_in
    qkv, z, b, a = jnp.split(proj, [GDN_CONV_DIM, GDN_CONV_DIM + GDN_V_DIM,
                                    GDN_CONV_DIM + GDN_V_DIM + GDN_V_HEADS], axis=-1)
    qkv = jax.nn.silu(causal_depthwise_conv(qkv, conv_w)).astype(jnp.float32)
    q, k, v = jnp.split(qkv, [GDN_QK_DIM, 2 * GDN_QK_DIM], axis=-1)
    rep = GDN_V_HEADS // GDN_QK_HEADS
    q = jnp.repeat(l2norm(q.reshape(Bb, S, GDN_QK_HEADS, HEAD_DIM)), rep, axis=2)
    k = jnp.repeat(l2norm(k.reshape(Bb, S, GDN_QK_HEADS, HEAD_DIM)), rep, axis=2)
    v = v.reshape(Bb, S, GDN_V_HEADS, HEAD_DIM)
    beta = jax.nn.sigmoid(b.astype(jnp.float32))
    g = -jnp.exp(a_log.astype(jnp.float32)) * jax.nn.softplus(a.astype(jnp.float32) + dt_bias.astype(jnp.float32))
    to_bhs = lambda t: jnp.moveaxis(t, 1, 2)
    o = gated_delta_rule(to_bhs(q), to_bhs(k), to_bhs(v), to_bhs(g), to_bhs(beta))
    o = jnp.moveaxis(o, 1, 2)
    o = rmsnorm(o, norm_w) * jax.nn.silu(z.astype(jnp.float32).reshape(Bb, S, GDN_V_HEADS, HEAD_DIM))
    return o.reshape(Bb, S, GDN_V_DIM).astype(h.dtype) @ w_out


def chunked_sgu(h, w_in, b_in, ln_g, ln_b, w_s, b_s, w_out, b_out):
    Bb, S, _ = h.shape
    n = S // SGU_CHUNK
    p = jax.nn.gelu(h @ w_in + b_in, approximate=False)
    u, v = jnp.split(p, 2, axis=-1)
    v = layernorm(v, ln_g, ln_b).reshape(Bb, n, SGU_CHUNK, SGU_GROUPS, SGU_GROUP_DIM)
    causal = jnp.tril(jnp.ones((SGU_CHUNK, SGU_CHUNK), dtype=bool))
    w_causal = jnp.where(causal, w_s, 0.0).astype(v.dtype)
    v = jnp.einsum('gts,bnsgc->bntgc', w_causal, v) + b_s.T.astype(v.dtype)[None, None, :, :, None]
    return (u * v.reshape(Bb, S, SGU_INNER)) @ w_out + b_out


def conformer_conv(h, w_pw1, b_pw1, w_dw, b_dw, ln_g, ln_b, w_pw2, b_pw2):
    val, gate = jnp.split(h @ w_pw1 + b_pw1, 2, axis=-1)
    y = val * jax.nn.sigmoid(gate)
    y = causal_depthwise_conv(y, w_dw) + b_dw
    y = jax.nn.silu(layernorm(y, ln_g, ln_b))
    return y @ w_pw2 + b_pw2


def swiglu(h, w_gate, w_up, w_down):
    return (jax.nn.silu(h @ w_gate) * (h @ w_up)) @ w_down


def moe_swiglu(h, router, w_gate, w_up, w_down):
    Bb, S, D = h.shape
    t = h.reshape(Bb * S, D)
    logits = t.astype(jnp.float32) @ router.astype(jnp.float32)
    top_logit, top_idx = lax.top_k(logits, TOP_K)
    top_w = jax.nn.softmax(top_logit, axis=-1)
    combine = jnp.sum(jax.nn.one_hot(top_idx, N_EXPERTS, dtype=jnp.float32) * top_w[..., None], axis=1)
    combine = combine.astype(t.dtype)
    y = jnp.zeros_like(t)
    for e in range(N_EXPERTS):
        y = y + combine[:, e:e + 1] * swiglu(t, w_gate[e], w_up[e], w_down[e])
    return y.reshape(Bb, S, D)


def setup_inputs(seed: int = 0) -> dict:
    key = jax.random.key(seed)
    ks = iter(jax.random.split(key, 48))
    D = D_MODEL

    def nrm(shape, scale):
        return jax.random.normal(next(ks), shape, jnp.float32) * scale

    def gain(shape):
        return 1.0 + nrm(shape, 0.02)

    return {
        'x': nrm((BATCH, SEQ, D), 1.0),
        'mix_norm': gain((DEPTH, D)),
        'ffn_norm': gain((DEPTH, D)),
        'gdn_w_in': nrm((N_GDN, D, GDN_IN_DIM), D ** -0.5),
        'gdn_conv_w': nrm((N_GDN, GDN_CONV, GDN_CONV_DIM), GDN_CONV ** -0.5),
        'gdn_a_log': jnp.log(jax.random.uniform(next(ks), (N_GDN, GDN_V_HEADS), jnp.float32, 1.0, 16.0)),
        'gdn_dt_bias': nrm((N_GDN, GDN_V_HEADS), 0.1),
        'gdn_norm': gain((N_GDN, HEAD_DIM)),
        'gdn_w_out': nrm((N_GDN, GDN_V_DIM, D), GDN_V_DIM ** -0.5),
        'sgu_w_in': nrm((N_SGU, D, 2 * SGU_INNER), D ** -0.5),
        'sgu_b_in': nrm((N_SGU, 2 * SGU_INNER), 0.02),
        'sgu_ln_g': gain((N_SGU, SGU_INNER)),
        'sgu_ln_b': nrm((N_SGU, SGU_INNER), 0.02),
        'sgu_w_s': nrm((N_SGU, SGU_GROUPS, SGU_CHUNK, SGU_CHUNK), SGU_CHUNK ** -0.5),
        'sgu_b_s': 1.0 + nrm((N_SGU, SGU_GROUPS, SGU_CHUNK), 0.02),
        'sgu_w_out': nrm((N_SGU, SGU_INNER, D), SGU_INNER ** -0.5),
        'sgu_b_out': nrm((N_SGU, D), 0.02),
        'cv_w_pw1': nrm((N_CONV, D, 2 * D), D ** -0.5),
        'cv_b_pw1': nrm((N_CONV, 2 * D), 0.02),
        'cv_w_dw': nrm((N_CONV, CONV_WIDTH, D), CONV_WIDTH ** -0.5),
        'cv_b_dw': nrm((N_CONV, D), 0.02),
        'cv_ln_g': gain((N_CONV, D)),
        'cv_ln_b': nrm((N_CONV, D), 0.02),
        'cv_w_pw2': nrm((N_CONV, D, D), D ** -0.5),
        'cv_b_pw2': nrm((N_CONV, D), 0.02),
        'ffn_w_gate': nrm((N_DENSE, D, FFN_DIM), D ** -0.5),
        'ffn_w_up': nrm((N_DENSE, D, FFN_DIM), D ** -0.5),
        'ffn_w_down': nrm((N_DENSE, FFN_DIM, D), FFN_DIM ** -0.5),
        'moe_router': nrm((N_MOE, D, N_EXPERTS), D ** -0.5),
        'moe_w_gate': nrm((N_MOE, N_EXPERTS, D, EXPERT_DIM), D ** -0.5),
        'moe_w_up': nrm((N_MOE, N_EXPERTS, D, EXPERT_DIM), D ** -0.5),
        'moe_w_down': nrm((N_MOE, N_EXPERTS, EXPERT_DIM, D), EXPERT_DIM ** -0.5),
        'final_norm': gain((D,)),
    }


def reference(x, mix_norm, ffn_norm,
              gdn_w_in, gdn_conv_w, gdn_a_log, gdn_dt_bias, gdn_norm, gdn_w_out,
              sgu_w_in, sgu_b_in, sgu_ln_g, sgu_ln_b, sgu_w_s, sgu_b_s, sgu_w_out, sgu_b_out,
              cv_w_pw1, cv_b_pw1, cv_w_dw, cv_b_dw, cv_ln_g, cv_ln_b, cv_w_pw2, cv_b_pw2,
              ffn_w_gate, ffn_w_up, ffn_w_down,
              moe_router, moe_w_gate, moe_w_up, moe_w_down,
              final_norm):
    for i in range(DEPTH):
        h = rmsnorm(x, mix_norm[i])
        kind, j = i % N_MIXERS, i // N_MIXERS
        if kind == 0:
            mix = gated_deltanet(h, gdn_w_in[j], gdn_conv_w[j], gdn_a_log[j], gdn_dt_bias[j],
                                 gdn_norm[j], gdn_w_out[j])
        elif kind == 1:
            mix = chunked_sgu(h, sgu_w_in[j], sgu_b_in[j], sgu_ln_g[j], sgu_ln_b[j], sgu_w_s[j],
                              sgu_b_s[j], sgu_w_out[j], sgu_b_out[j])
        else:
            mix = conformer_conv(h, cv_w_pw1[j], cv_b_pw1[j], cv_w_dw[j], cv_b_dw[j], cv_ln_g[j],
                                 cv_ln_b[j], cv_w_pw2[j], cv_b_pw2[j])
        x = x + mix
        h = rmsnorm(x, ffn_norm[i])
        j = i // 2
        if i % 2 == 0:
            ffn = swiglu(h, ffn_w_gate[j], ffn_w_up[j], ffn_w_down[j])
        else:
            ffn = moe_swiglu(h, moe_router[j], moe_w_gate[j], moe_w_up[j], moe_w_down[j])
        x = x + ffn
    return rmsnorm(x, final_norm)
```

```python
import functools

import jax
import jax.numpy as jnp
from jax import lax
from jax.experimental import pallas as pl
from jax.experimental.pallas import tpu as pltpu

BF16 = jnp.bfloat16
F32 = jnp.float32

D_MODEL = 2048
HEAD_DIM = 128
GDN_QK_HEADS = D_MODEL // HEAD_DIM
GDN_V_HEADS = 2 * GDN_QK_HEADS
GDN_QK_DIM = GDN_QK_HEADS * HEAD_DIM
GDN_V_DIM = GDN_V_HEADS * HEAD_DIM
GDN_CONV_DIM = 2 * GDN_QK_DIM + GDN_V_DIM
GDN_CONV = 4
GDN_CHUNK = 64
SGU_CHUNK = 128
SGU_INNER = 2 * D_MODEL
SGU_GROUPS = 16
SGU_GROUP_DIM = SGU_INNER // SGU_GROUPS
CONV_WIDTH = 31
N_EXPERTS = 8
N_MIXERS = 3
EPS = 1e-6
LN_EPS = 1e-5

LANES = 128
VMEM_LIMIT = 56 * 1024 * 1024
NEG_INF = float("-inf")


def _cparams(*sem):
    return pltpu.CompilerParams(dimension_semantics=sem, vmem_limit_bytes=VMEM_LIMIT)


def _silu(x):
    return x * jax.nn.sigmoid(x)


def _rms(x, g):
    return x * lax.rsqrt(jnp.mean(x * x, axis=-1, keepdims=True) + EPS) * g


def _rms_kernel(x_ref, g_ref, h_ref):
    h_ref[...] = _rms(x_ref[...], g_ref[...]).astype(h_ref.dtype)


def _add_rms_kernel(x_ref, d_ref, g_ref, xo_ref, h_ref):
    x = x_ref[...] + d_ref[...]
    xo_ref[...] = x
    h_ref[...] = _rms(x, g_ref[...]).astype(h_ref.dtype)


def _combine(x_ref, o1_ref, o2_ref, w_ref):
    w = w_ref[...]
    return x_ref[...] + (w[:, 0:1] * o1_ref[...] + w[:, 1:2] * o2_ref[...])


def _combine_rms_kernel(x_ref, o1_ref, o2_ref, w_ref, g_ref, xo_ref, h_ref):
    x = _combine(x_ref, o1_ref, o2_ref, w_ref)
    xo_ref[...] = x
    h_ref[...] = _rms(x, g_ref[...]).astype(h_ref.dtype)


def _combine_final_kernel(x_ref, o1_ref, o2_ref, w_ref, g_ref, h_ref):
    h_ref[...] = _rms(_combine(x_ref, o1_ref, o2_ref, w_ref), g_ref[...]).astype(h_ref.dtype)


def _row_spec(tm, d):
    return pl.BlockSpec((tm, d), lambda i: (i, 0))


def _vec_spec(d):
    return pl.BlockSpec((1, d), lambda i: (0, 0))


def rmsnorm(x, g, *, tm=512):
    m, d = x.shape
    return pl.pallas_call(
        _rms_kernel, grid=(m // tm,),
        in_specs=[_row_spec(tm, d), _vec_spec(d)],
        out_specs=_row_spec(tm, d),
        out_shape=jax.ShapeDtypeStruct((m, d), BF16),
        compiler_params=_cparams("parallel"), name="rmsnorm",
    )(x, g.reshape(1, d))


def add_rmsnorm(x, delta, g, *, tm=512):
    m, d = x.shape
    return pl.pallas_call(
        _add_rms_kernel, grid=(m // tm,),
        in_specs=[_row_spec(tm, d), _row_spec(tm, d), _vec_spec(d)],
        out_specs=[_row_spec(tm, d), _row_spec(tm, d)],
        out_shape=[jax.ShapeDtypeStruct((m, d), F32), jax.ShapeDtypeStruct((m, d), BF16)],
        compiler_params=_cparams("parallel"), name="add_rmsnorm",
    )(x, delta, g.reshape(1, d))


def combine_rmsnorm(x, o1, o2, wts, g, *, final, tm=512):
    m, d = x.shape
    in_specs = [_row_spec(tm, d), _row_spec(tm, d), _row_spec(tm, d), _row_spec(tm, LANES), _vec_spec(d)]
    if final:
        return pl.pallas_call(
            _combine_final_kernel, grid=(m // tm,), in_specs=in_specs,
            out_specs=_row_spec(tm, d), out_shape=jax.ShapeDtypeStruct((m, d), F32),
            compiler_params=_cparams("parallel"), name="combine_final_norm",
        )(x, o1, o2, wts, g.reshape(1, d))
    return pl.pallas_call(
        _combine_rms_kernel, grid=(m // tm,), in_specs=in_specs,
        out_specs=[_row_spec(tm, d), _row_spec(tm, d)],
        out_shape=[jax.ShapeDtypeStruct((m, d), F32), jax.ShapeDtypeStruct((m, d), BF16)],
        compiler_params=_cparams("parallel"), name="combine_rmsnorm",
    )(x, o1, o2, wts, g.reshape(1, d))


def _gelu_exact(x):
    return 0.5 * x * (1.0 + lax.erf(x * (2.0 ** -0.5)))


def _mm_kernel(*refs, nk, act, has_bias, has_res):
    x_ref, w_ref = refs[0], refs[1]
    pos = 2
    b_ref = r_ref = None
    if has_bias:
        b_ref = refs[pos]
        pos += 1
    if has_res:
        r_ref = refs[pos]
        pos += 1
    o_ref = refs[pos]
    acc_ref = refs[pos + 1] if nk > 1 else None

    def finalize(acc):
        if has_bias:
            acc = acc + b_ref[...]
        if act == "gelu":
            acc = _gelu_exact(acc)
        if has_res:
            acc = acc + r_ref[...]
        o_ref[...] = acc.astype(o_ref.dtype)

    part = jnp.dot(x_ref[...], w_ref[...], preferred_element_type=F32)
    if nk == 1:
        finalize(part)
        return
    k = pl.program_id(2)

    @pl.when(k == 0)
    def _():
        acc_ref[...] = part

    @pl.when(jnp.logical_and(k > 0, k < nk - 1))
    def _():
        acc_ref[...] += part

    @pl.when(k == nk - 1)
    def _():
        finalize(acc_ref[...] + part)


def matmul(x, w, *, bias=None, res=None, act=None, out_dtype=F32, tm=1024, tn=512, tk=2048):
    m, kd = x.shape
    n = w.shape[1]
    tm, tn, tk = min(tm, m), min(tn, n), min(tk, kd)
    nk = kd // tk
    in_specs = [pl.BlockSpec((tm, tk), lambda i, j, k: (i, k)),
                pl.BlockSpec((tk, tn), lambda i, j, k: (k, j))]
    args = [x, w]
    if bias is not None:
        in_specs.append(pl.BlockSpec((1, tn), lambda i, j, k: (0, j)))
        args.append(bias.reshape(1, n).astype(F32))
    if res is not None:
        in_specs.append(pl.BlockSpec((tm, tn), lambda i, j, k: (i, j)))
        args.append(res)
    return pl.pallas_call(
        functools.partial(_mm_kernel, nk=nk, act=act, has_bias=bias is not None, has_res=res is not None),
        grid=(m // tm, n // tn, nk),
        in_specs=in_specs,
        out_specs=pl.BlockSpec((tm, tn), lambda i, j, k: (i, j)),
        out_shape=jax.ShapeDtypeStruct((m, n), out_dtype),
        scratch_shapes=[pltpu.VMEM((tm, tn), F32)] if nk > 1 else [],
        compiler_params=_cparams("parallel", "parallel", "arbitrary"), name="matmul",
    )(*args)


def _glu_kernel(x_ref, wv_ref, wg_ref, bv_ref, bg_ref, o_ref):
    x = x_ref[...]
    val = jnp.dot(x, wv_ref[...], preferred_element_type=F32) + bv_ref[...]
    gate = jnp.dot(x, wg_ref[...], preferred_element_type=F32) + bg_ref[...]
    o_ref[...] = val * jax.nn.sigmoid(gate)


def glu_matmul(x, w, b, *, tm=1024, tn=512):
    m, kd = x.shape
    n = w.shape[1] // 2
    nb = n // tn
    b2 = b.reshape(1, 2 * n).astype(F32)
    return pl.pallas_call(
        _glu_kernel, grid=(m // tm, nb),
        in_specs=[pl.BlockSpec((tm, kd), lambda i, j: (i, 0)),
                  pl.BlockSpec((kd, tn), lambda i, j: (0, j)),
                  pl.BlockSpec((kd, tn), lambda i, j: (0, j + nb)),
                  pl.BlockSpec((1, tn), lambda i, j: (0, j)),
                  pl.BlockSpec((1, tn), lambda i, j: (0, j + nb))],
        out_specs=pl.BlockSpec((tm, tn), lambda i, j: (i, j)),
        out_shape=jax.ShapeDtypeStruct((m, n), F32),
        compiler_params=_cparams("parallel", "parallel"), name="glu_matmul",
    )(x, w, w, b2, b2)


def _ffn_kernel(te_ref, tv_ref, x_ref, wg_ref, wu_ref, wd_ref, o_ref, *scratch, cast_x):
    i = pl.program_id(0)
    f = pl.program_id(1)

    @pl.when(tv_ref[i] == 0)
    def _():
        @pl.when(f == 0)
        def _():
            o_ref[...] = jnp.zeros_like(o_ref)

    @pl.when(tv_ref[i] != 0)
    def _():
        if cast_x:
            xb_ref = scratch[0]

            @pl.when(f == 0)
            def _():
                xb_ref[...] = x_ref[...].astype(BF16)
            x = xb_ref[...]
        else:
            x = x_ref[...]
        g = jnp.dot(x, wg_ref[0], preferred_element_type=F32)
        u = jnp.dot(x, wu_ref[0], preferred_element_type=F32)
        a = (_silu(g) * u).astype(BF16)
        part = jnp.dot(a, wd_ref[0], preferred_element_type=F32)

        @pl.when(f == 0)
        def _():
            o_ref[...] = part

        @pl.when(f > 0)
        def _():
            o_ref[...] += part


def grouped_ffn(x, w_gate, w_up, w_down, tile_expert, tile_valid, *, tm, tf=512):
    mp, d = x.shape
    fdim = w_gate.shape[2]
    nf = fdim // tf
    cast_x = x.dtype != BF16

    def wcol(i, f, te, tv):
        return (te[i], 0, jnp.where(tv[i] != 0, f, nf - 1))

    def wrow(i, f, te, tv):
        return (te[i], jnp.where(tv[i] != 0, f, nf - 1), 0)

    grid_spec = pltpu.PrefetchScalarGridSpec(
        num_scalar_prefetch=2, grid=(mp // tm, nf),
        in_specs=[pl.BlockSpec((tm, d), lambda i, f, te, tv: (i, 0)),
                  pl.BlockSpec((1, d, tf), wcol),
                  pl.BlockSpec((1, d, tf), wcol),
                  pl.BlockSpec((1, tf, d), wrow)],
        out_specs=pl.BlockSpec((tm, d), lambda i, f, te, tv: (i, 0)),
        scratch_shapes=[pltpu.VMEM((tm, d), BF16)] if cast_x else [])
    return pl.pallas_call(
        functools.partial(_ffn_kernel, cast_x=cast_x), grid_spec=grid_spec,
        out_shape=jax.ShapeDtypeStruct((mp, d), F32),
        compiler_params=_cparams("parallel", "arbitrary"), name="grouped_ffn",
    )(tile_expert, tile_valid, x, w_gate, w_up, w_down)


def _split_bf16(x):
    hi = x.astype(BF16)
    lo = (x - hi.astype(F32)).astype(BF16)
    return hi, lo


def _router_kernel(x_ref, g_ref, rhi_ref, rlo_ref, h_ref, mi_ref, mf_ref, cnt_ref, carry_ref, *, tm):
    i = pl.program_id(0)

    @pl.when(i == 0)
    def _():
        carry_ref[...] = jnp.zeros_like(carry_ref)

    h = _rms(x_ref[...], g_ref[...])
    h_ref[...] = h
    h_hi, h_lo = _split_bf16(h)
    logits = (jnp.dot(h_hi, rhi_ref[...], preferred_element_type=F32)
              + (jnp.dot(h_lo, rhi_ref[...], preferred_element_type=F32)
                 + jnp.dot(h_hi, rlo_ref[...], preferred_element_type=F32)))
    lane = lax.broadcasted_iota(jnp.int32, (tm, LANES), 1)
    lg = jnp.where(lane < N_EXPERTS, logits, NEG_INF)
    m1 = jnp.max(lg, axis=1, keepdims=True)
    i1 = jnp.min(jnp.where(lg == m1, lane, LANES), axis=1, keepdims=True)
    lg2 = jnp.where(lane == i1, NEG_INF, lg)
    m2 = jnp.max(lg2, axis=1, keepdims=True)
    i2 = jnp.min(jnp.where(lg2 == m2, lane, LANES), axis=1, keepdims=True)
    e2 = jnp.exp(m2 - m1)
    w1 = 1.0 / (1.0 + e2)
    w2 = e2 / (1.0 + e2)
    oh1 = (lane == i1).astype(F32)
    oh2 = (lane == i2).astype(F32)
    tot = oh1 + oh2
    r = lax.broadcasted_iota(jnp.int32, (tm, tm), 0)
    c = lax.broadcasted_iota(jnp.int32, (tm, tm), 1)
    before = jnp.dot((c < r).astype(BF16), tot.astype(BF16), preferred_element_type=F32)
    base = before + carry_ref[0:1, :]
    rank1 = jnp.sum(oh1 * base, axis=1, keepdims=True).astype(jnp.int32)
    rank2 = jnp.sum(oh2 * base, axis=1, keepdims=True).astype(jnp.int32)
    mi_ref[...] = jnp.where(lane == 0, i1, jnp.where(lane == 1, i2, jnp.where(lane == 2, rank1, rank2)))
    mf_ref[...] = jnp.where(lane == 0, w1, w2)
    carry = carry_ref[...] + jnp.sum(tot, axis=0, keepdims=True)
    carry_ref[...] = carry
    cnt_ref[...] = carry


def route(x, g, router, *, tm=512):
    m, d = x.shape
    rpad = jnp.zeros((d, LANES), F32).at[:, :N_EXPERTS].set(router.astype(F32))
    rhi, rlo = _split_bf16(rpad)
    return pl.pallas_call(
        functools.partial(_router_kernel, tm=tm), grid=(m // tm,),
        in_specs=[_row_spec(tm, d), _vec_spec(d),
                  pl.BlockSpec((d, LANES), lambda i: (0, 0)), pl.BlockSpec((d, LANES), lambda i: (0, 0))],
        out_specs=[_row_spec(tm, d), _row_spec(tm, LANES), _row_spec(tm, LANES),
                   pl.BlockSpec((8, LANES), lambda i: (0, 0))],
        out_shape=[jax.ShapeDtypeStruct((m, d), F32), jax.ShapeDtypeStruct((m, LANES), jnp.int32),
                   jax.ShapeDtypeStruct((m, LANES), F32), jax.ShapeDtypeStruct((8, LANES), F32)],
        scratch_shapes=[pltpu.VMEM((8, LANES), F32)],
        compiler_params=_cparams("arbitrary"), name="moe_route",
    )(x, g.reshape(1, d), rhi, rlo)


def _row_copy_kernel(sidx_ref, didx_ref, src_ref, *rest, rows, aliased):
    dst_ref, sem = (rest[1], rest[2]) if aliased else (rest[0], rest[1])

    def copy(r):
        return pltpu.make_async_copy(src_ref.at[pl.ds(sidx_ref[r], 1)], dst_ref.at[pl.ds(didx_ref[r], 1)], sem)

    def start(r, carry):
        copy(r).start()
        return carry

    def wait(r, carry):
        copy(r).wait()
        return carry

    lax.fori_loop(0, rows, start, 0)
    lax.fori_loop(0, rows, wait, 0)


def row_copy(src, sidx, didx, *, dst_rows=None, dst_init=None, rows=512):
    n = sidx.shape[0]
    d = src.shape[1]
    aliased = dst_init is not None
    if aliased:
        dst_rows = dst_init.shape[0]
    idx_spec = pl.BlockSpec((rows,), lambda i: (i,), memory_space=pltpu.SMEM)
    any_spec = pl.BlockSpec(memory_space=pl.ANY)
    in_specs = [idx_spec, idx_spec, any_spec] + ([any_spec] if aliased else [])
    args = [sidx, didx, src] + ([dst_init] if aliased else [])
    return pl.pallas_call(
        functools.partial(_row_copy_kernel, rows=rows, aliased=aliased), grid=(n // rows,),
        in_specs=in_specs, out_specs=any_spec,
        out_shape=jax.ShapeDtypeStruct((dst_rows, d), src.dtype),
        scratch_shapes=[pltpu.SemaphoreType.DMA(())],
        input_output_aliases={3: 0} if aliased else {},
        compiler_params=pltpu.CompilerParams(dimension_semantics=("arbitrary",), has_side_effects=True),
        name="row_copy",
    )(*args)


def moe_layer(x, g, router, w_gate, w_up, w_down, *, tm=1024):
    n, d = x.shape
    h, meta_i, wts, counts = route(x, g, router)
    e1, e2, r1, r2 = (meta_i[:, k] for k in range(4))
    cnt = counts[0, :N_EXPERTS].astype(jnp.int32)
    padded = ((cnt + tm - 1) // tm) * tm
    ends = jnp.cumsum(padded)
    starts = ends - padded
    pos1 = starts[e1] + r1
    pos2 = starts[e2] + r2
    n_tiles = (2 * n) // tm + N_EXPERTS
    tile_start = jnp.arange(n_tiles, dtype=jnp.int32) * tm
    tile_expert = jnp.minimum(jnp.sum((tile_start[:, None] >= ends[None, :]).astype(jnp.int32), axis=1),
                              N_EXPERTS - 1)
    tile_valid = (tile_start < ends[-1]).astype(jnp.int32)
    tok = jnp.arange(n, dtype=jnp.int32)
    xs = row_copy(h, jnp.concatenate([tok, tok]), jnp.concatenate([pos1, pos2]),
                  dst_init=jnp.zeros((n_tiles * tm, d), F32))
    ys = grouped_ffn(xs, w_gate, w_up, w_down, tile_expert, tile_valid, tm=tm)
    o = row_copy(ys, jnp.concatenate([pos1, pos2]), jnp.arange(2 * n, dtype=jnp.int32), dst_rows=2 * n)
    return o[:n], o[n:], wts


def _inv_unit_lower(lmat, row, col):
    def mm(a, b):
        return jnp.dot(a.astype(BF16), b.astype(BF16), preferred_element_type=F32)

    eye = (row == col).astype(F32)
    same16 = (row // 16) == (col // 16)
    same32 = (row // 32) == (col // 32)
    nd = jnp.where(same16, -lmat, 0.0)
    x = eye + nd
    p = mm(nd, nd)
    x = x + mm(x, p)
    p = mm(p, p)
    x = x + mm(x, p)
    p = mm(p, p)
    x = x + mm(x, p)
    off32 = jnp.where(jnp.logical_and(same32, jnp.logical_not(same16)), lmat, 0.0)
    x = x - mm(mm(x, off32), x)
    off64 = jnp.where(same32, 0.0, lmat)
    return x - mm(mm(x, off64), x)


def _gdn_kernel(q_ref, k_ref, v_ref, z_ref, ba_ref, cwq_ref, cwk_ref, cwv_ref, alog_ref, dt_ref, nw_ref,
                o_ref, state_ref, hq_ref, hk_ref, hv_ref, xq_ref, xk_ref, xv_ref, *, ts):
    p = pl.program_id(1)
    s = pl.program_id(2)
    hd = HEAD_DIM

    @pl.when(s == 0)
    def _():
        state_ref[...] = jnp.zeros_like(state_ref)
        hq_ref[...] = jnp.zeros_like(hq_ref)
        hk_ref[...] = jnp.zeros_like(hk_ref)
        hv_ref[...] = jnp.zeros_like(hv_ref)

    def conv_silu(x_ref, hist_ref, xx_ref, w_ref):
        xx_ref[0:8, :] = hist_ref[...]
        xx_ref[8:, :] = x_ref[...]
        hist_ref[...] = x_ref[ts - 8:, :]
        w = w_ref[...]
        acc = w[0:1, :] * xx_ref[5:5 + ts, :]
        for j in range(1, GDN_CONV):
            acc = acc + w[j:j + 1, :] * xx_ref[5 + j:5 + j + ts, :]
        return _silu(acc)

    def l2n(x):
        return x * lax.rsqrt(jnp.sum(x * x, axis=-1, keepdims=True) + EPS)

    q = l2n(conv_silu(q_ref, hq_ref, xq_ref, cwq_ref)) * (hd ** -0.5)
    k = l2n(conv_silu(k_ref, hk_ref, xk_ref, cwk_ref))
    v = conv_silu(v_ref, hv_ref, xv_ref, cwv_ref)

    ba = ba_ref[...]
    beta_all = jax.nn.sigmoid(ba)
    sp = ba + dt_ref[...]
    softplus = jnp.maximum(sp, 0.0) + jnp.log1p(jnp.exp(-jnp.abs(sp)))
    g_all = -jnp.exp(alog_ref[...]) * softplus
    rowi = lax.broadcasted_iota(jnp.int32, (ts, LANES), 0)
    inchunk = rowi % GDN_CHUNK
    gc_all = g_all
    sh = 1
    while sh < GDN_CHUNK:
        gc_all = gc_all + jnp.where(inchunk >= sh, pltpu.roll(gc_all, sh, 0), 0.0)
        sh *= 2
    lane = lax.broadcasted_iota(jnp.int32, (ts, LANES), 1)

    def pick(x, idx):
        return jnp.sum(jnp.where(lane == idx, x, 0.0), axis=1, keepdims=True)

    beta = [pick(beta_all, 2 * p + h) for h in range(2)]
    gc = [pick(gc_all, GDN_V_HEADS + 2 * p + h) for h in range(2)]
    egc = [jnp.exp(x) for x in gc]

    r128 = lax.broadcasted_iota(jnp.int32, (2 * GDN_CHUNK, 2 * GDN_CHUNK), 0)
    c128 = lax.broadcasted_iota(jnp.int32, (2 * GDN_CHUNK, 2 * GDN_CHUNK), 1)
    same_chunk = (r128 // GDN_CHUNK) == (c128 // GDN_CHUNK)
    incl = jnp.logical_and(same_chunk, c128 <= r128)
    strict = jnp.logical_and(same_chunk, c128 < r128)

    qb = q.astype(BF16)
    kb = k.astype(BF16)
    sc = 2 * GDN_CHUNK
    outs = []
    for u in range(ts // sc):
        rows = slice(u * sc, (u + 1) * sc)
        q_u, k_u = qb[rows], kb[rows]
        qk_u = jnp.concatenate([q_u, k_u], axis=0)
        ab = lax.dot_general(qk_u, k_u, (((1,), (1,)), ((), ())), preferred_element_type=F32)
        qkt, kkt = ab[:sc], ab[sc:]
        kt_u = k[rows].T.astype(BF16)
        tat = []
        for h in range(2):
            gcol = gc[h][rows]
            grow = jnp.broadcast_to(gcol, (sc, sc)).T
            decay = jnp.exp(jnp.where(incl, gcol - grow, NEG_INF))
            lmat = jnp.where(strict, kkt * beta[h][rows] * decay, 0.0)
            attn = qkt * decay
            tmat = _inv_unit_lower(lmat, r128, c128)
            at = jnp.dot(attn.astype(BF16), tmat.astype(BF16), preferred_element_type=F32)
            tat.append((tmat.astype(BF16), at.astype(BF16)))
        o_parts = []
        for c in range(2):
            crow = slice(u * sc + c * GDN_CHUNK, u * sc + (c + 1) * GDN_CHUNK)
            lrow = slice(c * GDN_CHUNK, (c + 1) * GDN_CHUNK)
            last = u * sc + (c + 1) * GDN_CHUNK - 1
            state = state_ref[...]
            qks = jnp.dot(jnp.concatenate([qb[crow], kb[crow]], axis=0), state.astype(BF16),
                          preferred_element_type=F32)
            zeros = jnp.zeros((GDN_CHUNK, hd), F32)
            o_h, vnt = [], []
            for h in range(2):
                hs = slice(h * hd, (h + 1) * hd)
                qs, ks = qks[:GDN_CHUNK, hs], qks[GDN_CHUNK:, hs]
                rmat = beta[h][crow] * (v[crow, hs] - egc[h][crow] * ks)
                rfull = jnp.concatenate([rmat, zeros] if c == 0 else [zeros, rmat], axis=0)
                lhs = jnp.concatenate([tat[h][0][lrow], tat[h][1][lrow]], axis=0)
                res = jnp.dot(lhs, rfull.astype(BF16), preferred_element_type=F32)
                v_new, attn_o = res[:GDN_CHUNK], res[GDN_CHUNK:]
                o_h.append(egc[h][crow] * qs + attn_o)
                gl = gc[h][last:last + 1]
                vnt.append(jnp.exp(gl - gc[h][crow]) * v_new)
            vn2 = jnp.concatenate(vnt, axis=1)
            z2 = jnp.zeros_like(vn2)
            vfull = jnp.concatenate([vn2, z2] if c == 0 else [z2, vn2], axis=0)
            upd = jnp.dot(kt_u, vfull.astype(BF16), preferred_element_type=F32)
            lastg = [jnp.exp(gc[h][last:last + 1]) for h in range(2)]
            scale = jnp.concatenate([jnp.broadcast_to(lastg[0], (1, hd)),
                                     jnp.broadcast_to(lastg[1], (1, hd))], axis=1)
            state_ref[...] = state * scale + upd
            o_parts.append(jnp.concatenate(o_h, axis=1))
        outs.extend(o_parts)
    o = jnp.concatenate(outs, axis=0)

    z = z_ref[...]
    nw = nw_ref[...]
    ys = []
    for h in range(2):
        hs = slice(h * hd, (h + 1) * hd)
        ys.append(_rms(o[:, hs], nw) * _silu(z[:, hs]))
    o_ref[...] = jnp.concatenate(ys, axis=1).astype(o_ref.dtype)


def gdn_core(proj, ba, conv_w, a_log, dt_bias, norm_w, *, batch, seq, ts=512):
    hd = HEAD_DIM
    ns = seq // ts
    n_pairs = GDN_QK_HEADS
    kq = GDN_QK_DIM // hd
    vb = (2 * GDN_QK_DIM) // (2 * hd)
    zb = GDN_CONV_DIM // (2 * hd)
    alog_row = jnp.zeros((1, LANES), F32).at[0, GDN_V_HEADS:2 * GDN_V_HEADS].set(a_log.astype(F32))
    dt_row = jnp.zeros((1, LANES), F32).at[0, GDN_V_HEADS:2 * GDN_V_HEADS].set(dt_bias.astype(F32))

    def rows(b, p, s):
        return b * ns + s

    in_specs = [
        pl.BlockSpec((ts, hd), lambda b, p, s: (rows(b, p, s), p)),
        pl.BlockSpec((ts, hd), lambda b, p, s: (rows(b, p, s), kq + p)),
        pl.BlockSpec((ts, 2 * hd), lambda b, p, s: (rows(b, p, s), vb + p)),
        pl.BlockSpec((ts, 2 * hd), lambda b, p, s: (rows(b, p, s), zb + p)),
        pl.BlockSpec((ts, LANES), lambda b, p, s: (rows(b, p, s), 0)),
        pl.BlockSpec((GDN_CONV, hd), lambda b, p, s: (0, p)),
        pl.BlockSpec((GDN_CONV, hd), lambda b, p, s: (0, kq + p)),
        pl.BlockSpec((GDN_CONV, 2 * hd), lambda b, p, s: (0, vb + p)),
        pl.BlockSpec((1, LANES), lambda b, p, s: (0, 0)),
        pl.BlockSpec((1, LANES), lambda b, p, s: (0, 0)),
        pl.BlockSpec((1, hd), lambda b, p, s: (0, 0)),
    ]
    return pl.pallas_call(
        functools.partial(_gdn_kernel, ts=ts), grid=(batch, n_pairs, ns),
        in_specs=in_specs,
        out_specs=pl.BlockSpec((ts, 2 * hd), lambda b, p, s: (rows(b, p, s), p)),
        out_shape=jax.ShapeDtypeStruct((batch * seq, GDN_V_DIM), BF16),
        scratch_shapes=[pltpu.VMEM((hd, 2 * hd), F32),
                        pltpu.VMEM((8, hd), F32), pltpu.VMEM((8, hd), F32), pltpu.VMEM((8, 2 * hd), F32),
                        pltpu.VMEM((ts + 8, hd), F32), pltpu.VMEM((ts + 8, hd), F32),
                        pltpu.VMEM((ts + 8, 2 * hd), F32)],
        compiler_params=_cparams("parallel", "parallel", "arbitrary"), name="gdn_core",
    )(proj, proj, proj, proj, ba, conv_w, conv_w, conv_w, alog_row, dt_row, norm_w.reshape(1, hd).astype(F32))


def gated_deltanet(x, h, w_in, conv_w, a_log, dt_bias, norm_w, w_out, *, batch, seq):
    main = GDN_CONV_DIM + GDN_V_DIM
    w_main = w_in[:, :main].astype(BF16)
    w_ba = jnp.zeros((w_in.shape[0], LANES), F32).at[:, :2 * GDN_V_HEADS].set(w_in[:, main:]).astype(BF16)
    proj = matmul(h, w_main)
    ba = matmul(h, w_ba)
    y = gdn_core(proj, ba, conv_w.astype(F32), a_log, dt_bias, norm_w, batch=batch, seq=seq)
    return matmul(y, w_out.astype(BF16), res=x)


def _sgu_kernel(u_ref, v_ref, g_ref, b_ref, ws_ref, bs_ref, o_ref, *, nch):
    r = lax.broadcasted_iota(jnp.int32, (SGU_CHUNK, SGU_CHUNK), 0)
    c = lax.broadcasted_iota(jnp.int32, (SGU_CHUNK, SGU_CHUNK), 1)
    causal = c <= r
    gd = SGU_GROUP_DIM
    for ch in range(nch):
        rows = slice(ch * SGU_CHUNK, (ch + 1) * SGU_CHUNK)
        v = v_ref[rows, :]
        mu = jnp.mean(v, axis=-1, keepdims=True)
        vc = v - mu
        vn = vc * lax.rsqrt(jnp.mean(vc * vc, axis=-1, keepdims=True) + LN_EPS)
        vn = (vn * g_ref[...] + b_ref[...]).astype(BF16)
        for grp in range(SGU_GROUPS):
            cols = slice(grp * gd, (grp + 1) * gd)
            w = jnp.where(causal, ws_ref[grp], 0.0).astype(BF16)
            sv = jnp.dot(w, vn[:, cols], preferred_element_type=F32) + bs_ref[:, grp:grp + 1]
            o_ref[rows, cols] = (u_ref[rows, cols] * sv).astype(o_ref.dtype)


def sgu_spatial(p, ln_g, ln_b, w_s, b_s, *, nch=2):
    m = p.shape[0]
    rows = nch * SGU_CHUNK
    inner = SGU_INNER
    return pl.pallas_call(
        functools.partial(_sgu_kernel, nch=nch), grid=(m // rows,),
        in_specs=[pl.BlockSpec((rows, inner), lambda i: (i, 0)),
                  pl.BlockSpec((rows, inner), lambda i: (i, 1)),
                  _vec_spec(inner), _vec_spec(inner),
                  pl.BlockSpec((SGU_GROUPS, SGU_CHUNK, SGU_CHUNK), lambda i: (0, 0, 0)),
                  pl.BlockSpec((SGU_CHUNK, SGU_GROUPS), lambda i: (0, 0))],
        out_specs=pl.BlockSpec((rows, inner), lambda i: (i, 0)),
        out_shape=jax.ShapeDtypeStruct((m, inner), BF16),
        compiler_params=_cparams("parallel"), name="sgu_spatial",
    )(p, p, ln_g.reshape(1, inner).astype(F32), ln_b.reshape(1, inner).astype(F32),
      w_s.astype(F32), b_s.T.astype(F32))


def chunked_sgu(x, h, w_in, b_in, ln_g, ln_b, w_s, b_s, w_out, b_out):
    p = matmul(h, w_in.astype(BF16), bias=b_in, act="gelu")
    uv = sgu_spatial(p, ln_g, ln_b, w_s, b_s)
    return matmul(uv, w_out.astype(BF16), bias=b_out, res=x)


CV_HALO = 32


def _cvconv_kernel(halo_ref, y_ref, w_ref, bdw_ref, g_ref, b_ref, o_ref, xx_ref, yc_ref, *, ts, cb):
    s = pl.program_id(1)
    d = y_ref.shape[1]

    @pl.when(s == 0)
    def _():
        xx_ref[0:CV_HALO, :] = jnp.zeros((CV_HALO, d), F32)

    @pl.when(s > 0)
    def _():
        xx_ref[0:CV_HALO, :] = halo_ref[...]

    xx_ref[CV_HALO:, :] = y_ref[...]
    off = CV_HALO - (CONV_WIDTH - 1)
    for blk in range(d // cb):
        cols = slice(blk * cb, (blk + 1) * cb)
        w = w_ref[:, cols]
        acc = w[0:1, :] * xx_ref[off:off + ts, cols]
        for j in range(1, CONV_WIDTH):
            acc = acc + w[j:j + 1, :] * xx_ref[off + j:off + j + ts, cols]
        yc_ref[:, cols] = acc + bdw_ref[:, cols]
    y = yc_ref[...]
    mu = jnp.mean(y, axis=-1, keepdims=True)
    yc = y - mu
    yn = yc * lax.rsqrt(jnp.mean(yc * yc, axis=-1, keepdims=True) + LN_EPS)
    o_ref[...] = _silu(yn * g_ref[...] + b_ref[...]).astype(o_ref.dtype)


def conformer_dwconv(y, w_dw, b_dw, ln_g, ln_b, *, batch, seq, ts=128, cb=256):
    m, d = y.shape
    ns = seq // ts
    hb = ts // CV_HALO
    return pl.pallas_call(
        functools.partial(_cvconv_kernel, ts=ts, cb=cb), grid=(batch, ns),
        in_specs=[pl.BlockSpec((CV_HALO, d), lambda b, s: (jnp.maximum((b * ns + s) * hb - 1, 0), 0)),
                  pl.BlockSpec((ts, d), lambda b, s: (b * ns + s, 0)),
                  pl.BlockSpec((CONV_WIDTH, d), lambda b, s: (0, 0)),
                  pl.BlockSpec((1, d), lambda b, s: (0, 0)),
                  pl.BlockSpec((1, d), lambda b, s: (0, 0)),
                  pl.BlockSpec((1, d), lambda b, s: (0, 0))],
        out_specs=pl.BlockSpec((ts, d), lambda b, s: (b * ns + s, 0)),
        out_shape=jax.ShapeDtypeStruct((m, d), BF16),
        scratch_shapes=[pltpu.VMEM((ts + CV_HALO, d), F32), pltpu.VMEM((ts, d), F32)],
        compiler_params=_cparams("parallel", "parallel"), name="conformer_dwconv",
    )(y, y, w_dw.astype(F32), b_dw.reshape(1, d).astype(F32), ln_g.reshape(1, d).astype(F32),
      ln_b.reshape(1, d).astype(F32))


def conformer_conv(x, h, w_pw1, b_pw1, w_dw, b_dw, ln_g, ln_b, w_pw2, b_pw2, *, batch, seq):
    y = glu_matmul(h, w_pw1.astype(BF16), b_pw1)
    yn = conformer_dwconv(y, w_dw, b_dw, ln_g, ln_b, batch=batch, seq=seq)
    return matmul(yn, w_pw2.astype(BF16), bias=b_pw2, res=x)


def dense_ffn(h, w_gate, w_up, w_down, *, tm=1024):
    m = h.shape[0]
    nt = m // tm
    return grouped_ffn(h, w_gate[None].astype(BF16), w_up[None].astype(BF16), w_down[None].astype(BF16),
                       jnp.zeros((nt,), jnp.int32), jnp.ones((nt,), jnp.int32), tm=tm)


def kernel(x, mix_norm, ffn_norm, gdn_w_in, gdn_conv_w, gdn_a_log, gdn_dt_bias, gdn_norm, gdn_w_out, sgu_w_in, sgu_b_in, sgu_ln_g, sgu_ln_b, sgu_w_s, sgu_b_s, sgu_w_out, sgu_b_out, cv_w_pw1, cv_b_pw1, cv_w_dw, cv_b_dw, cv_ln_g, cv_ln_b, cv_w_pw2, cv_b_pw2, ffn_w_gate, ffn_w_up, ffn_w_down, moe_router, moe_w_gate, moe_w_up, moe_w_down, final_norm):
    batch, seq, d = x.shape
    depth = mix_norm.shape[0]
    xr = x.reshape(batch * seq, d).astype(F32)
    pending = None
    for i in range(depth):
        if pending is None:
            h = rmsnorm(xr, mix_norm[i])
        elif pending[0] == "dense":
            xr, h = add_rmsnorm(xr, pending[1], mix_norm[i])
        else:
            xr, h = combine_rmsnorm(xr, *pending[1:], mix_norm[i], final=False)
        kind, j = i % N_MIXERS, i // N_MIXERS
        if kind == 0:
            xr = gated_deltanet(xr, h, gdn_w_in[j], gdn_conv_w[j], gdn_a_log[j], gdn_dt_bias[j], gdn_norm[j],
                                gdn_w_out[j], batch=batch, seq=seq)
        elif kind == 1:
            xr = chunked_sgu(xr, h, sgu_w_in[j], sgu_b_in[j], sgu_ln_g[j], sgu_ln_b[j], sgu_w_s[j], sgu_b_s[j],
                             sgu_w_out[j], sgu_b_out[j])
        else:
            xr = conformer_conv(xr, h, cv_w_pw1[j], cv_b_pw1[j], cv_w_dw[j], cv_b_dw[j], cv_ln_g[j], cv_ln_b[j],
                                cv_w_pw2[j], cv_b_pw2[j], batch=batch, seq=seq)
        j = i // 2
        if i % 2 == 0:
            h = rmsnorm(xr, ffn_norm[i])
            pending = ("dense", dense_ffn(h, ffn_w_gate[j], ffn_w_up[j], ffn_w_down[j]))
        else:
            pending = ("moe",) + moe_layer(xr, ffn_norm[i], moe_router[j], moe_w_gate[j].astype(BF16),
                                           moe_w_up[j].astype(BF16), moe_w_down[j].astype(BF16))
    if pending[0] == "dense":
        xr, _ = add_rmsnorm(xr, pending[1], final_norm)
        out = _final_rms(xr, final_norm)
    else:
        out = combine_rmsnorm(xr, *pending[1:], final_norm, final=True)
    return out.reshape(batch, seq, d).astype(x.dtype)


def _final_rms(x, g, *, tm=512):
    m, d = x.shape
    return pl.pallas_call(
        _rms_kernel, grid=(m // tm,),
        in_specs=[_row_spec(tm, d), _vec_spec(d)],
        out_specs=_row_spec(tm, d),
        out_shape=jax.ShapeDtypeStruct((m, d), F32),
        compiler_params=_cparams("parallel"), name="final_rmsnorm",
    )(x, g.reshape(1, d))
```

```python
import functools

import jax
import jax.numpy as jnp
from jax import lax
from jax.experimental import pallas as pl
from jax.experimental.pallas import tpu as pltpu

BF16 = jnp.bfloat16
F32 = jnp.float32

D_MODEL = 2048
HEAD_DIM = 128
GDN_QK_HEADS = D_MODEL // HEAD_DIM
GDN_V_HEADS = 2 * GDN_QK_HEADS
GDN_QK_DIM = GDN_QK_HEADS * HEAD_DIM
GDN_V_DIM = GDN_V_HEADS * HEAD_DIM
GDN_CONV_DIM = 2 * GDN_QK_DIM + GDN_V_DIM
GDN_CONV = 4
GDN_CHUNK = 64
SGU_CHUNK = 128
SGU_INNER = 2 * D_MODEL
SGU_GROUPS = 16
SGU_GROUP_DIM = SGU_INNER // SGU_GROUPS
CONV_WIDTH = 31
N_EXPERTS = 8
N_MIXERS = 3
EPS = 1e-6
LN_EPS = 1e-5

LANES = 128
ROW_CHUNKS = D_MODEL // LANES
VMEM_LIMIT = 56 * 1024 * 1024
NEG_INF = float("-inf")


def _cparams(*sem):
    return pltpu.CompilerParams(dimension_semantics=sem, vmem_limit_bytes=VMEM_LIMIT)


def _silu(x):
    return x * jax.nn.sigmoid(x)


def _rms(x, g):
    return x * lax.rsqrt(jnp.mean(x * x, axis=-1, keepdims=True) + EPS) * g


def _rms_kernel(x_ref, g_ref, h_ref):
    h_ref[...] = _rms(x_ref[...], g_ref[...]).astype(h_ref.dtype)


def _add_rms_kernel(x_ref, d_ref, g_ref, xo_ref, h_ref):
    x = x_ref[...] + d_ref[...]
    xo_ref[...] = x
    h_ref[...] = _rms(x, g_ref[...]).astype(h_ref.dtype)


def _chunk_rows(c, tm):
    return pl.ds(c, tm, stride=ROW_CHUNKS)


def _combine(x_ref, o1_ref, o2_ref, w_ref):
    tm = x_ref.shape[0]
    w = w_ref[...]
    w1, w2 = w[:, 0:1], w[:, 1:2]
    parts = []
    for c in range(ROW_CHUNKS):
        rows = _chunk_rows(c, tm)
        parts.append(x_ref[:, c * LANES:(c + 1) * LANES] + (w1 * o1_ref[rows, :] + w2 * o2_ref[rows, :]))
    return jnp.concatenate(parts, axis=1)


def _combine_rms_kernel(x_ref, o1_ref, o2_ref, w_ref, g_ref, xo_ref, h_ref):
    x = _combine(x_ref, o1_ref, o2_ref, w_ref)
    xo_ref[...] = x
    h_ref[...] = _rms(x, g_ref[...]).astype(h_ref.dtype)


def _combine_final_kernel(x_ref, o1_ref, o2_ref, w_ref, g_ref, h_ref):
    h_ref[...] = _rms(_combine(x_ref, o1_ref, o2_ref, w_ref), g_ref[...]).astype(h_ref.dtype)


def _row_spec(tm, d):
    return pl.BlockSpec((tm, d), lambda i: (i, 0))


def _vec_spec(d):
    return pl.BlockSpec((1, d), lambda i: (0, 0))


def rmsnorm(x, g, *, tm=512):
    m, d = x.shape
    return pl.pallas_call(
        _rms_kernel, grid=(m // tm,),
        in_specs=[_row_spec(tm, d), _vec_spec(d)],
        out_specs=_row_spec(tm, d),
        out_shape=jax.ShapeDtypeStruct((m, d), BF16),
        compiler_params=_cparams("parallel"), name="rmsnorm",
    )(x, g.reshape(1, d))


def add_rmsnorm(x, delta, g, *, tm=512):
    m, d = x.shape
    return pl.pallas_call(
        _add_rms_kernel, grid=(m // tm,),
        in_specs=[_row_spec(tm, d), _row_spec(tm, d), _vec_spec(d)],
        out_specs=[_row_spec(tm, d), _row_spec(tm, d)],
        out_shape=[jax.ShapeDtypeStruct((m, d), F32), jax.ShapeDtypeStruct((m, d), BF16)],
        compiler_params=_cparams("parallel"), name="add_rmsnorm",
    )(x, delta, g.reshape(1, d))


def combine_rmsnorm(x, o, wts, g, *, final, tm=256):
    m, d = x.shape
    nt = m // tm
    in_specs = [_row_spec(tm, d),
                pl.BlockSpec((tm * ROW_CHUNKS, LANES), lambda i: (i, 0)),
                pl.BlockSpec((tm * ROW_CHUNKS, LANES), lambda i: (i + nt, 0)),
                _row_spec(tm, LANES), _vec_spec(d)]
    if final:
        return pl.pallas_call(
            _combine_final_kernel, grid=(nt,), in_specs=in_specs,
            out_specs=_row_spec(tm, d), out_shape=jax.ShapeDtypeStruct((m, d), F32),
            compiler_params=_cparams("parallel"), name="combine_final_norm",
        )(x, o, o, wts, g.reshape(1, d))
    return pl.pallas_call(
        _combine_rms_kernel, grid=(nt,), in_specs=in_specs,
        out_specs=[_row_spec(tm, d), _row_spec(tm, d)],
        out_shape=[jax.ShapeDtypeStruct((m, d), F32), jax.ShapeDtypeStruct((m, d), BF16)],
        compiler_params=_cparams("parallel"), name="combine_rmsnorm",
    )(x, o, o, wts, g.reshape(1, d))


def _gelu_exact(x):
    return 0.5 * x * (1.0 + lax.erf(x * (2.0 ** -0.5)))


def _mm_kernel(*refs, nk, act, has_bias, has_res):
    x_ref, w_ref = refs[0], refs[1]
    pos = 2
    b_ref = r_ref = None
    if has_bias:
        b_ref = refs[pos]
        pos += 1
    if has_res:
        r_ref = refs[pos]
        pos += 1
    o_ref = refs[pos]
    acc_ref = refs[pos + 1] if nk > 1 else None

    def finalize(acc):
        if has_bias:
            acc = acc + b_ref[...]
        if act == "gelu":
            acc = _gelu_exact(acc)
        if has_res:
            acc = acc + r_ref[...]
        o_ref[...] = acc.astype(o_ref.dtype)

    part = jnp.dot(x_ref[...], w_ref[...], preferred_element_type=F32)
    if nk == 1:
        finalize(part)
        return
    k = pl.program_id(2)

    @pl.when(k == 0)
    def _():
        acc_ref[...] = part

    @pl.when(jnp.logical_and(k > 0, k < nk - 1))
    def _():
        acc_ref[...] += part

    @pl.when(k == nk - 1)
    def _():
        finalize(acc_ref[...] + part)


def matmul(x, w, *, bias=None, res=None, act=None, out_dtype=F32, tm=1024, tn=512, tk=2048):
    m, kd = x.shape
    n = w.shape[1]
    tm, tn, tk = min(tm, m), min(tn, n), min(tk, kd)
    nk = kd // tk
    in_specs = [pl.BlockSpec((tm, tk), lambda i, j, k: (i, k)),
                pl.BlockSpec((tk, tn), lambda i, j, k: (k, j))]
    args = [x, w]
    if bias is not None:
        in_specs.append(pl.BlockSpec((1, tn), lambda i, j, k: (0, j)))
        args.append(bias.reshape(1, n).astype(F32))
    if res is not None:
        in_specs.append(pl.BlockSpec((tm, tn), lambda i, j, k: (i, j)))
        args.append(res)
    return pl.pallas_call(
        functools.partial(_mm_kernel, nk=nk, act=act, has_bias=bias is not None, has_res=res is not None),
        grid=(m // tm, n // tn, nk),
        in_specs=in_specs,
        out_specs=pl.BlockSpec((tm, tn), lambda i, j, k: (i, j)),
        out_shape=jax.ShapeDtypeStruct((m, n), out_dtype),
        scratch_shapes=[pltpu.VMEM((tm, tn), F32)] if nk > 1 else [],
        compiler_params=_cparams("parallel", "parallel", "arbitrary"), name="matmul",
    )(*args)


def _glu_kernel(x_ref, wv_ref, wg_ref, bv_ref, bg_ref, o_ref):
    x = x_ref[...]
    val = jnp.dot(x, wv_ref[...], preferred_element_type=F32) + bv_ref[...]
    gate = jnp.dot(x, wg_ref[...], preferred_element_type=F32) + bg_ref[...]
    o_ref[...] = val * jax.nn.sigmoid(gate)


def glu_matmul(x, w, b, *, tm=1024, tn=512):
    m, kd = x.shape
    n = w.shape[1] // 2
    nb = n // tn
    b2 = b.reshape(1, 2 * n).astype(F32)
    return pl.pallas_call(
        _glu_kernel, grid=(m // tm, nb),
        in_specs=[pl.BlockSpec((tm, kd), lambda i, j: (i, 0)),
                  pl.BlockSpec((kd, tn), lambda i, j: (0, j)),
                  pl.BlockSpec((kd, tn), lambda i, j: (0, j + nb)),
                  pl.BlockSpec((1, tn), lambda i, j: (0, j)),
                  pl.BlockSpec((1, tn), lambda i, j: (0, j + nb))],
        out_specs=pl.BlockSpec((tm, tn), lambda i, j: (i, j)),
        out_shape=jax.ShapeDtypeStruct((m, n), F32),
        compiler_params=_cparams("parallel", "parallel"), name="glu_matmul",
    )(x, w, w, b2, b2)


def _swiglu_part(x, wg_ref, wu_ref, wd_ref):
    g = jnp.dot(x, wg_ref[0], preferred_element_type=F32)
    u = jnp.dot(x, wu_ref[0], preferred_element_type=F32)
    a = (_silu(g) * u).astype(BF16)
    return jnp.dot(a, wd_ref[0], preferred_element_type=F32)


def _ffn_kernel(te_ref, tv_ref, x_ref, wg_ref, wu_ref, wd_ref, o_ref):
    f = pl.program_id(1)
    part = _swiglu_part(x_ref[...], wg_ref, wu_ref, wd_ref)

    @pl.when(f == 0)
    def _():
        o_ref[...] = part

    @pl.when(f > 0)
    def _():
        o_ref[...] += part


def _ffn_chunkrow_kernel(te_ref, tv_ref, x_ref, wg_ref, wu_ref, wd_ref, o_ref, xb_ref, acc_ref, *, tm, nf):
    i = pl.program_id(0)
    f = pl.program_id(1)

    @pl.when(jnp.logical_and(tv_ref[i] == 0, f == 0))
    def _():
        o_ref[...] = jnp.zeros_like(o_ref)

    @pl.when(tv_ref[i] != 0)
    def _():
        @pl.when(f == 0)
        def _():
            for c in range(ROW_CHUNKS):
                xb_ref[:, c * LANES:(c + 1) * LANES] = x_ref[_chunk_rows(c, tm), :].astype(BF16)

        part = _swiglu_part(xb_ref[...], wg_ref, wu_ref, wd_ref)

        @pl.when(f == 0)
        def _():
            acc_ref[...] = part

        @pl.when(jnp.logical_and(f > 0, f < nf - 1))
        def _():
            acc_ref[...] += part

        @pl.when(f == nf - 1)
        def _():
            for c in range(ROW_CHUNKS):
                cols = slice(c * LANES, (c + 1) * LANES)
                o_ref[_chunk_rows(c, tm), :] = acc_ref[:, cols] + part[:, cols]


def grouped_ffn(x, w_gate, w_up, w_down, tile_expert, tile_valid, *, tm, tf=512, chunk_rows=False):
    d = w_gate.shape[1]
    mp = x.shape[0] // ROW_CHUNKS if chunk_rows else x.shape[0]
    fdim = w_gate.shape[2]
    nf = fdim // tf

    def wcol(i, f, te, tv):
        return (te[i], 0, jnp.where(tv[i] != 0, f, nf - 1))

    def wrow(i, f, te, tv):
        return (te[i], jnp.where(tv[i] != 0, f, nf - 1), 0)

    if chunk_rows:
        row_block = pl.BlockSpec((tm * ROW_CHUNKS, LANES), lambda i, f, te, tv: (i, 0))
        body = functools.partial(_ffn_chunkrow_kernel, tm=tm, nf=nf)
        scratch = [pltpu.VMEM((tm, d), BF16), pltpu.VMEM((tm, d), F32)]
        out_shape = jax.ShapeDtypeStruct((mp * ROW_CHUNKS, LANES), F32)
    else:
        row_block = pl.BlockSpec((tm, d), lambda i, f, te, tv: (i, 0))
        body = _ffn_kernel
        scratch = []
        out_shape = jax.ShapeDtypeStruct((mp, d), F32)
    grid_spec = pltpu.PrefetchScalarGridSpec(
        num_scalar_prefetch=2, grid=(mp // tm, nf),
        in_specs=[row_block,
                  pl.BlockSpec((1, d, tf), wcol),
                  pl.BlockSpec((1, d, tf), wcol),
                  pl.BlockSpec((1, tf, d), wrow)],
        out_specs=row_block, scratch_shapes=scratch)
    return pl.pallas_call(
        body, grid_spec=grid_spec, out_shape=out_shape,
        compiler_params=_cparams("parallel", "arbitrary"), name="grouped_ffn",
    )(tile_expert, tile_valid, x, w_gate, w_up, w_down)


def _split_bf16(x):
    hi = x.astype(BF16)
    lo = (x - hi.astype(F32)).astype(BF16)
    return hi, lo


def _router_kernel(x_ref, g_ref, rhi_ref, rlo_ref, h_ref, mi_ref, mf_ref, cnt_ref, carry_ref, *, tm):
    i = pl.program_id(0)

    @pl.when(i == 0)
    def _():
        carry_ref[...] = jnp.zeros_like(carry_ref)

    h = _rms(x_ref[...], g_ref[...])
    for c in range(ROW_CHUNKS):
        h_ref[_chunk_rows(c, tm), :] = h[:, c * LANES:(c + 1) * LANES]
    h_hi, h_lo = _split_bf16(h)
    logits = (jnp.dot(h_hi, rhi_ref[...], preferred_element_type=F32)
              + (jnp.dot(h_lo, rhi_ref[...], preferred_element_type=F32)
                 + jnp.dot(h_hi, rlo_ref[...], preferred_element_type=F32)))
    lane = lax.broadcasted_iota(jnp.int32, (tm, LANES), 1)
    lg = jnp.where(lane < N_EXPERTS, logits, NEG_INF)
    m1 = jnp.max(lg, axis=1, keepdims=True)
    i1 = jnp.min(jnp.where(lg == m1, lane, LANES), axis=1, keepdims=True)
    lg2 = jnp.where(lane == i1, NEG_INF, lg)
    m2 = jnp.max(lg2, axis=1, keepdims=True)
    i2 = jnp.min(jnp.where(lg2 == m2, lane, LANES), axis=1, keepdims=True)
    e2 = jnp.exp(m2 - m1)
    w1 = 1.0 / (1.0 + e2)
    w2 = e2 / (1.0 + e2)
    oh1 = (lane == i1).astype(F32)
    oh2 = (lane == i2).astype(F32)
    tot = oh1 + oh2
    r = lax.broadcasted_iota(jnp.int32, (tm, tm), 0)
    c = lax.broadcasted_iota(jnp.int32, (tm, tm), 1)
    before = jnp.dot((c < r).astype(BF16), tot.astype(BF16), preferred_element_type=F32)
    base = before + carry_ref[0:1, :]
    rank1 = jnp.sum(oh1 * base, axis=1, keepdims=True).astype(jnp.int32)
    rank2 = jnp.sum(oh2 * base, axis=1, keepdims=True).astype(jnp.int32)
    mi_ref[...] = jnp.where(lane == 0, i1, jnp.where(lane == 1, i2, jnp.where(lane == 2, rank1, rank2)))
    mf_ref[...] = jnp.where(lane == 0, w1, w2)
    carry = carry_ref[...] + jnp.sum(tot, axis=0, keepdims=True)
    carry_ref[...] = carry
    cnt_ref[...] = carry


def route(x, g, router, *, tm=512):
    m, d = x.shape
    rpad = jnp.zeros((d, LANES), F32).at[:, :N_EXPERTS].set(router.astype(F32))
    rhi, rlo = _split_bf16(rpad)
    return pl.pallas_call(
        functools.partial(_router_kernel, tm=tm), grid=(m // tm,),
        in_specs=[_row_spec(tm, d), _vec_spec(d),
                  pl.BlockSpec((d, LANES), lambda i: (0, 0)), pl.BlockSpec((d, LANES), lambda i: (0, 0))],
        out_specs=[_row_spec(tm * ROW_CHUNKS, LANES), _row_spec(tm, LANES), _row_spec(tm, LANES),
                   pl.BlockSpec((8, LANES), lambda i: (0, 0))],
        out_shape=[jax.ShapeDtypeStruct((m * ROW_CHUNKS, LANES), F32), jax.ShapeDtypeStruct((m, LANES), jnp.int32),
                   jax.ShapeDtypeStruct((m, LANES), F32), jax.ShapeDtypeStruct((8, LANES), F32)],
        scratch_shapes=[pltpu.VMEM((8, LANES), F32)],
        compiler_params=_cparams("arbitrary"), name="moe_route",
    )(x, g.reshape(1, d), rhi, rlo)


def _row_copy_kernel(sidx_ref, didx_ref, src_ref, *rest, rows, aliased):
    dst_ref, sem = (rest[1], rest[2]) if aliased else (rest[0], rest[1])

    def copy(r):
        s = pl.multiple_of(sidx_ref[r] * ROW_CHUNKS, ROW_CHUNKS)
        t = pl.multiple_of(didx_ref[r] * ROW_CHUNKS, ROW_CHUNKS)
        return pltpu.make_async_copy(src_ref.at[pl.ds(s, ROW_CHUNKS)], dst_ref.at[pl.ds(t, ROW_CHUNKS)], sem)

    def start(r, carry):
        copy(r).start()
        return carry

    def wait(r, carry):
        copy(r).wait()
        return carry

    lax.fori_loop(0, rows, start, 0, unroll=8)
    lax.fori_loop(0, rows, wait, 0, unroll=8)


def row_copy(src, sidx, didx, *, dst_rows=None, dst_init=None, rows=1024):
    n = sidx.shape[0]
    aliased = dst_init is not None
    if aliased:
        dst_rows = dst_init.shape[0] // ROW_CHUNKS
    idx_spec = pl.BlockSpec((rows,), lambda i: (i,), memory_space=pltpu.SMEM)
    any_spec = pl.BlockSpec(memory_space=pl.ANY)
    in_specs = [idx_spec, idx_spec, any_spec] + ([any_spec] if aliased else [])
    args = [sidx, didx, src] + ([dst_init] if aliased else [])
    return pl.pallas_call(
        functools.partial(_row_copy_kernel, rows=rows, aliased=aliased), grid=(n // rows,),
        in_specs=in_specs, out_specs=any_spec,
        out_shape=jax.ShapeDtypeStruct((dst_rows * ROW_CHUNKS, LANES), src.dtype),
        scratch_shapes=[pltpu.SemaphoreType.DMA(())],
        input_output_aliases={3: 0} if aliased else {},
        compiler_params=pltpu.CompilerParams(dimension_semantics=("arbitrary",), has_side_effects=True),
        name="row_copy",
    )(*args)


def moe_layer(x, g, router, w_gate, w_up, w_down, *, tm=512):
    n, d = x.shape
    h, meta_i, wts, counts = route(x, g, router)
    e1, e2, r1, r2 = (meta_i[:, k] for k in range(4))
    cnt = counts[0, :N_EXPERTS].astype(jnp.int32)
    padded = ((cnt + tm - 1) // tm) * tm
    ends = jnp.cumsum(padded)
    starts = ends - padded
    pos1 = starts[e1] + r1
    pos2 = starts[e2] + r2
    n_tiles = (2 * n) // tm + N_EXPERTS
    tile_start = jnp.arange(n_tiles, dtype=jnp.int32) * tm
    tile_expert = jnp.minimum(jnp.sum((tile_start[:, None] >= ends[None, :]).astype(jnp.int32), axis=1),
                              N_EXPERTS - 1)
    tile_valid = (tile_start < ends[-1]).astype(jnp.int32)
    tok = jnp.arange(n, dtype=jnp.int32)
    pos = jnp.concatenate([pos1, pos2])
    xs = row_copy(h, jnp.concatenate([tok, tok]), pos,
                  dst_init=jnp.zeros((n_tiles * tm * ROW_CHUNKS, LANES), F32))
    ys = grouped_ffn(xs, w_gate, w_up, w_down, tile_expert, tile_valid, tm=tm, chunk_rows=True)
    o = row_copy(ys, pos, jnp.arange(2 * n, dtype=jnp.int32), dst_rows=2 * n)
    return o, wts


def _gdn_gates_kernel(ba_ref, alog_ref, dt_ref, o_ref, *, tm):
    ba = ba_ref[...]
    sp = ba + dt_ref[...]
    softplus = jnp.maximum(sp, 0.0) + jnp.log1p(jnp.exp(-jnp.abs(sp)))
    gcum = -jnp.exp(alog_ref[...]) * softplus
    inchunk = lax.broadcasted_iota(jnp.int32, (tm, LANES), 0) % GDN_CHUNK
    sh = 1
    while sh < GDN_CHUNK:
        gcum = gcum + jnp.where(inchunk >= sh, pltpu.roll(gcum, sh, 0), 0.0)
        sh *= 2
    lane = lax.broadcasted_iota(jnp.int32, (tm, LANES), 1)
    o_ref[...] = jnp.where(lane < GDN_V_HEADS, jax.nn.sigmoid(ba), gcum)


def gdn_gates(ba, a_log, dt_bias, *, tm=512):
    m = ba.shape[0]
    alog_row = jnp.zeros((1, LANES), F32).at[0, GDN_V_HEADS:2 * GDN_V_HEADS].set(a_log.astype(F32))
    dt_row = jnp.zeros((1, LANES), F32).at[0, GDN_V_HEADS:2 * GDN_V_HEADS].set(dt_bias.astype(F32))
    return pl.pallas_call(
        functools.partial(_gdn_gates_kernel, tm=tm), grid=(m // tm,),
        in_specs=[_row_spec(tm, LANES), _vec_spec(LANES), _vec_spec(LANES)],
        out_specs=_row_spec(tm, LANES), out_shape=jax.ShapeDtypeStruct((m, LANES), F32),
        compiler_params=_cparams("parallel"), name="gdn_gates",
    )(ba, alog_row, dt_row)


def _gdn_kernel(q_ref, k_ref, v_ref, z_ref, gates_ref, cwq_ref, cwk_ref, cwv_ref, nw_ref,
                o_ref, state_ref, hq_ref, hk_ref, hv_ref, xq_ref, xk_ref, xv_ref, *, ts):
    p = pl.program_id(1)
    s = pl.program_id(2)
    hd = HEAD_DIM

    @pl.when(s == 0)
    def _():
        state_ref[...] = jnp.zeros_like(state_ref)
        hq_ref[...] = jnp.zeros_like(hq_ref)
        hk_ref[...] = jnp.zeros_like(hk_ref)
        hv_ref[...] = jnp.zeros_like(hv_ref)

    def conv_silu(x_ref, hist_ref, xx_ref, w_ref):
        xx_ref[0:8, :] = hist_ref[...]
        xx_ref[8:, :] = x_ref[...]
        hist_ref[...] = x_ref[ts - 8:, :]
        w = w_ref[...]
        acc = w[0:1, :] * xx_ref[5:5 + ts, :]
        for j in range(1, GDN_CONV):
            acc = acc + w[j:j + 1, :] * xx_ref[5 + j:5 + j + ts, :]
        return _silu(acc)

    def l2n(x):
        return x * lax.rsqrt(jnp.sum(x * x, axis=-1, keepdims=True) + EPS)

    q = l2n(conv_silu(q_ref, hq_ref, xq_ref, cwq_ref)) * (hd ** -0.5)
    k = l2n(conv_silu(k_ref, hk_ref, xk_ref, cwk_ref))
    v = conv_silu(v_ref, hv_ref, xv_ref, cwv_ref)

    gates = gates_ref[...]
    lane = lax.broadcasted_iota(jnp.int32, (ts, LANES), 1)

    def pick(idx):
        return jnp.sum(jnp.where(lane == idx, gates, 0.0), axis=1, keepdims=True)

    beta = [pick(2 * p + h) for h in range(2)]
    gc = [pick(GDN_V_HEADS + 2 * p + h) for h in range(2)]
    egc = [jnp.exp(x) for x in gc]

    sc = 2 * GDN_CHUNK
    r128 = lax.broadcasted_iota(jnp.int32, (sc, sc), 0)
    c128 = lax.broadcasted_iota(jnp.int32, (sc, sc), 1)
    same_chunk = (r128 // GDN_CHUNK) == (c128 // GDN_CHUNK)
    incl = jnp.logical_and(same_chunk, c128 <= r128)
    strict = jnp.logical_and(same_chunk, c128 < r128)

    def mm(a, b):
        return jnp.dot(a.astype(BF16), b.astype(BF16), preferred_element_type=F32)

    qb = q.astype(BF16)
    kb = k.astype(BF16)
    n_blk = ts // sc
    chains = [(u, h) for u in range(n_blk) for h in range(2)]
    blk_rows = [slice(u * sc, (u + 1) * sc) for u in range(n_blk)]
    ab = [lax.dot_general(jnp.concatenate([qb[r], kb[r]], axis=0), kb[r], (((1,), (1,)), ((), ())),
                          preferred_element_type=F32) for r in blk_rows]
    kt = [k[r].T.astype(BF16) for r in blk_rows]
    eye = (r128 == c128).astype(F32)
    same16 = (r128 // 16) == (c128 // 16)
    same32 = (r128 // 32) == (c128 // 32)
    lmat, attn, ymat = [], [], []
    for u, h in chains:
        r = blk_rows[u]
        gcol, bcol, ecol = gc[h][r], beta[h][r], egc[h][r]
        grow = jnp.broadcast_to(gcol, (sc, sc)).T
        decay = jnp.exp(jnp.where(incl, gcol - grow, NEG_INF))
        lmat.append(jnp.where(strict, ab[u][sc:] * bcol * decay, 0.0))
        attn.append(ab[u][:sc] * decay)
        ymat.append(jnp.concatenate([bcol * v[r, h * hd:(h + 1) * hd], (bcol * ecol) * k[r]], axis=1))
    nd = [jnp.where(same16, -m, 0.0) for m in lmat]
    x = [eye + m for m in nd]
    pw = [mm(m, m) for m in nd]
    for level in range(3):
        xp = [mm(a, b) for a, b in zip(x, pw)]
        if level < 2:
            pw = [mm(m, m) for m in pw]
        x = [a + b for a, b in zip(x, xp)]
    for inner, outer in ((same16, same32), (same32, same_chunk)):
        off = [jnp.where(jnp.logical_and(outer, jnp.logical_not(inner)), m, 0.0) for m in lmat]
        t1 = [mm(a, b) for a, b in zip(x, off)]
        t2 = [mm(a, b) for a, b in zip(t1, x)]
        x = [a - b for a, b in zip(x, t2)]
    uw = [mm(a, b) for a, b in zip(x, ymat)]
    dw = [mm(a, b) for a, b in zip(attn, uw)]
    steps = [[] for _ in range(2)]
    for i, (u, h) in enumerate(chains):
        r = blk_rows[u]
        cmat = egc[h][r] * q[r] - dw[i][:, hd:]
        for c in range(2):
            lrow = slice(c * GDN_CHUNK, (c + 1) * GDN_CHUNK)
            last = u * sc + (c + 1) * GDN_CHUNK - 1
            gl = gc[h][last:last + 1]
            tail = jnp.exp(gl - gc[h][r][lrow]) * uw[i][lrow]
            zeros = jnp.zeros_like(tail)
            full = jnp.concatenate([tail, zeros] if c == 0 else [zeros, tail], axis=0)
            bm = jnp.dot(kt[u], full.astype(BF16), preferred_element_type=F32)
            mc = jnp.concatenate([bm[:, hd:], cmat[lrow]], axis=0).astype(BF16)
            steps[h].append((mc, bm[:, :hd], dw[i][lrow, :hd], jnp.exp(gl)))

    states = [state_ref[h] for h in range(2)]
    outs = [[] for _ in range(2)]
    for cc in range(ts // GDN_CHUNK):
        for h in range(2):
            mc, bmat, dmat, glast = steps[h][cc]
            ms = jnp.dot(mc, states[h].astype(BF16), preferred_element_type=F32)
            outs[h].append(ms[hd:] + dmat)
            states[h] = glast * states[h] + (bmat - ms[:hd])
    for h in range(2):
        state_ref[h] = states[h]
    o = jnp.concatenate([jnp.concatenate(outs[h], axis=0) for h in range(2)], axis=1)

    z = z_ref[...]
    nw = nw_ref[...]
    ys = []
    for h in range(2):
        hs = slice(h * hd, (h + 1) * hd)
        ys.append(_rms(o[:, hs], nw) * _silu(z[:, hs]))
    o_ref[...] = jnp.concatenate(ys, axis=1).astype(o_ref.dtype)


def gdn_core(proj, gates, conv_w, norm_w, *, batch, seq, ts=512):
    hd = HEAD_DIM
    ns = seq // ts
    n_pairs = GDN_QK_HEADS
    kq = GDN_QK_DIM // hd
    vb = (2 * GDN_QK_DIM) // (2 * hd)
    zb = GDN_CONV_DIM // (2 * hd)

    def rows(b, p, s):
        return b * ns + s

    in_specs = [
        pl.BlockSpec((ts, hd), lambda b, p, s: (rows(b, p, s), p)),
        pl.BlockSpec((ts, hd), lambda b, p, s: (rows(b, p, s), kq + p)),
        pl.BlockSpec((ts, 2 * hd), lambda b, p, s: (rows(b, p, s), vb + p)),
        pl.BlockSpec((ts, 2 * hd), lambda b, p, s: (rows(b, p, s), zb + p)),
        pl.BlockSpec((ts, LANES), lambda b, p, s: (rows(b, p, s), 0)),
        pl.BlockSpec((GDN_CONV, hd), lambda b, p, s: (0, p)),
        pl.BlockSpec((GDN_CONV, hd), lambda b, p, s: (0, kq + p)),
        pl.BlockSpec((GDN_CONV, 2 * hd), lambda b, p, s: (0, vb + p)),
        pl.BlockSpec((1, hd), lambda b, p, s: (0, 0)),
    ]
    return pl.pallas_call(
        functools.partial(_gdn_kernel, ts=ts), grid=(batch, n_pairs, ns),
        in_specs=in_specs,
        out_specs=pl.BlockSpec((ts, 2 * hd), lambda b, p, s: (rows(b, p, s), p)),
        out_shape=jax.ShapeDtypeStruct((batch * seq, GDN_V_DIM), BF16),
        scratch_shapes=[pltpu.VMEM((2, hd, hd), F32),
                        pltpu.VMEM((8, hd), F32), pltpu.VMEM((8, hd), F32), pltpu.VMEM((8, 2 * hd), F32),
                        pltpu.VMEM((ts + 8, hd), F32), pltpu.VMEM((ts + 8, hd), F32),
                        pltpu.VMEM((ts + 8, 2 * hd), F32)],
        compiler_params=_cparams("parallel", "parallel", "arbitrary"), name="gdn_core",
    )(proj, proj, proj, proj, gates, conv_w, conv_w, conv_w, norm_w.reshape(1, hd).astype(F32))


def gated_deltanet(x, h, w_in, conv_w, a_log, dt_bias, norm_w, w_out, *, batch, seq):
    main = GDN_CONV_DIM + GDN_V_DIM
    w_main = w_in[:, :main].astype(BF16)
    w_ba = jnp.zeros((w_in.shape[0], LANES), F32).at[:, :2 * GDN_V_HEADS].set(w_in[:, main:]).astype(BF16)
    proj = matmul(h, w_main)
    gates = gdn_gates(matmul(h, w_ba), a_log, dt_bias)
    y = gdn_core(proj, gates, conv_w.astype(F32), norm_w, batch=batch, seq=seq)
    return matmul(y, w_out.astype(BF16), res=x)


def _sgu_kernel(u_ref, v_ref, g_ref, b_ref, ws_ref, bs_ref, o_ref, *, nch):
    r = lax.broadcasted_iota(jnp.int32, (SGU_CHUNK, SGU_CHUNK), 0)
    c = lax.broadcasted_iota(jnp.int32, (SGU_CHUNK, SGU_CHUNK), 1)
    causal = c <= r
    gd = SGU_GROUP_DIM
    for ch in range(nch):
        rows = slice(ch * SGU_CHUNK, (ch + 1) * SGU_CHUNK)
        v = v_ref[rows, :]
        mu = jnp.mean(v, axis=-1, keepdims=True)
        vc = v - mu
        vn = vc * lax.rsqrt(jnp.mean(vc * vc, axis=-1, keepdims=True) + LN_EPS)
        vn = (vn * g_ref[...] + b_ref[...]).astype(BF16)
        for grp in range(SGU_GROUPS):
            cols = slice(grp * gd, (grp + 1) * gd)
            w = jnp.where(causal, ws_ref[grp], 0.0).astype(BF16)
            sv = jnp.dot(w, vn[:, cols], preferred_element_type=F32) + bs_ref[:, grp:grp + 1]
            o_ref[rows, cols] = (u_ref[rows, cols] * sv).astype(o_ref.dtype)


def sgu_spatial(p, ln_g, ln_b, w_s, b_s, *, nch=2):
    m = p.shape[0]
    rows = nch * SGU_CHUNK
    inner = SGU_INNER
    return pl.pallas_call(
        functools.partial(_sgu_kernel, nch=nch), grid=(m // rows,),
        in_specs=[pl.BlockSpec((rows, inner), lambda i: (i, 0)),
                  pl.BlockSpec((rows, inner), lambda i: (i, 1)),
                  _vec_spec(inner), _vec_spec(inner),
                  pl.BlockSpec((SGU_GROUPS, SGU_CHUNK, SGU_CHUNK), lambda i: (0, 0, 0)),
                  pl.BlockSpec((SGU_CHUNK, SGU_GROUPS), lambda i: (0, 0))],
        out_specs=pl.BlockSpec((rows, inner), lambda i: (i, 0)),
        out_shape=jax.ShapeDtypeStruct((m, inner), BF16),
        compiler_params=_cparams("parallel"), name="sgu_spatial",
    )(p, p, ln_g.reshape(1, inner).astype(F32), ln_b.reshape(1, inner).astype(F32),
      w_s.astype(F32), b_s.T.astype(F32))


def chunked_sgu(x, h, w_in, b_in, ln_g, ln_b, w_s, b_s, w_out, b_out):
    p = matmul(h, w_in.astype(BF16), bias=b_in, act="gelu")
    uv = sgu_spatial(p, ln_g, ln_b, w_s, b_s)
    return matmul(uv, w_out.astype(BF16), bias=b_out, res=x)


CV_HALO = 32


def _cvconv_kernel(halo_ref, y_ref, w_ref, bdw_ref, g_ref, b_ref, o_ref, xx_ref, yc_ref, *, ts, cb):
    s = pl.program_id(1)
    d = y_ref.shape[1]

    @pl.when(s == 0)
    def _():
        xx_ref[0:CV_HALO, :] = jnp.zeros((CV_HALO, d), F32)

    @pl.when(s > 0)
    def _():
        xx_ref[0:CV_HALO, :] = halo_ref[...]

    xx_ref[CV_HALO:, :] = y_ref[...]
    off = CV_HALO - (CONV_WIDTH - 1)
    for blk in range(d // cb):
        cols = slice(blk * cb, (blk + 1) * cb)
        w = w_ref[:, cols]
        acc = w[0:1, :] * xx_ref[off:off + ts, cols]
        for j in range(1, CONV_WIDTH):
            acc = acc + w[j:j + 1, :] * xx_ref[off + j:off + j + ts, cols]
        yc_ref[:, cols] = acc + bdw_ref[:, cols]
    y = yc_ref[...]
    mu = jnp.mean(y, axis=-1, keepdims=True)
    yc = y - mu
    yn = yc * lax.rsqrt(jnp.mean(yc * yc, axis=-1, keepdims=True) + LN_EPS)
    o_ref[...] = _silu(yn * g_ref[...] + b_ref[...]).astype(o_ref.dtype)


def conformer_dwconv(y, w_dw, b_dw, ln_g, ln_b, *, batch, seq, ts=128, cb=256):
    m, d = y.shape
    ns = seq // ts
    hb = ts // CV_HALO
    return pl.pallas_call(
        functools.partial(_cvconv_kernel, ts=ts, cb=cb), grid=(batch, ns),
        in_specs=[pl.BlockSpec((CV_HALO, d), lambda b, s: (jnp.maximum((b * ns + s) * hb - 1, 0), 0)),
                  pl.BlockSpec((ts, d), lambda b, s: (b * ns + s, 0)),
                  pl.BlockSpec((CONV_WIDTH, d), lambda b, s: (0, 0)),
                  pl.BlockSpec((1, d), lambda b, s: (0, 0)),
                  pl.BlockSpec((1, d), lambda b, s: (0, 0)),
                  pl.BlockSpec((1, d), lambda b, s: (0, 0))],
        out_specs=pl.BlockSpec((ts, d), lambda b, s: (b * ns + s, 0)),
        out_shape=jax.ShapeDtypeStruct((m, d), BF16),
        scratch_shapes=[pltpu.VMEM((ts + CV_HALO, d), F32), pltpu.VMEM((ts, d), F32)],
        compiler_params=_cparams("parallel", "parallel"), name="conformer_dwconv",
    )(y, y, w_dw.astype(F32), b_dw.reshape(1, d).astype(F32), ln_g.reshape(1, d).astype(F32),
      ln_b.reshape(1, d).astype(F32))


def conformer_conv(x, h, w_pw1, b_pw1, w_dw, b_dw, ln_g, ln_b, w_pw2, b_pw2, *, batch, seq):
    y = glu_matmul(h, w_pw1.astype(BF16), b_pw1)
    yn = conformer_dwconv(y, w_dw, b_dw, ln_g, ln_b, batch=batch, seq=seq)
    return matmul(yn, w_pw2.astype(BF16), bias=b_pw2, res=x)


def dense_ffn(h, w_gate, w_up, w_down, *, tm=1024):
    m = h.shape[0]
    nt = m // tm
    return grouped_ffn(h, w_gate[None].astype(BF16), w_up[None].astype(BF16), w_down[None].astype(BF16),
                       jnp.zeros((nt,), jnp.int32), jnp.ones((nt,), jnp.int32), tm=tm)


def kernel(x, mix_norm, ffn_norm, gdn_w_in, gdn_conv_w, gdn_a_log, gdn_dt_bias, gdn_norm, gdn_w_out, sgu_w_in, sgu_b_in, sgu_ln_g, sgu_ln_b, sgu_w_s, sgu_b_s, sgu_w_out, sgu_b_out, cv_w_pw1, cv_b_pw1, cv_w_dw, cv_b_dw, cv_ln_g, cv_ln_b, cv_w_pw2, cv_b_pw2, ffn_w_gate, ffn_w_up, ffn_w_down, moe_router, moe_w_gate, moe_w_up, moe_w_down, final_norm):
    batch, seq, d = x.shape
    depth = mix_norm.shape[0]
    xr = x.reshape(batch * seq, d).astype(F32)
    pending = None
    for i in range(depth):
        if pending is None:
            h = rmsnorm(xr, mix_norm[i])
        elif pending[0] == "dense":
            xr, h = add_rmsnorm(xr, pending[1], mix_norm[i])
        else:
            xr, h = combine_rmsnorm(xr, *pending[1:], mix_norm[i], final=False)
        kind, j = i % N_MIXERS, i // N_MIXERS
        if kind == 0:
            xr = gated_deltanet(xr, h, gdn_w_in[j], gdn_conv_w[j], gdn_a_log[j], gdn_dt_bias[j], gdn_norm[j],
                                gdn_w_out[j], batch=batch, seq=seq)
        elif kind == 1:
            xr = chunked_sgu(xr, h, sgu_w_in[j], sgu_b_in[j], sgu_ln_g[j], sgu_ln_b[j], sgu_w_s[j], sgu_b_s[j],
                             sgu_w_out[j], sgu_b_out[j])
        else:
            xr = conformer_conv(xr, h, cv_w_pw1[j], cv_b_pw1[j], cv_w_dw[j], cv_b_dw[j], cv_ln_g[j], cv_ln_b[j],
                                cv_w_pw2[j], cv_b_pw2[j], batch=batch, seq=seq)
        j = i // 2
        if i % 2 == 0:
            h = rmsnorm(xr, ffn_norm[i])
            pending = ("dense", dense_ffn(h, ffn_w_gate[j], ffn_w_up[j], ffn_w_down[j]))
        else:
            pending = ("moe",) + moe_layer(xr, ffn_norm[i], moe_router[j], moe_w_gate[j].astype(BF16),
                                           moe_w_up[j].astype(BF16), moe_w_down[j].astype(BF16))
    if pending[0] == "dense":
        xr, _ = add_rmsnorm(xr, pending[1], final_norm)
        out = _final_rms(xr, final_norm)
    else:
        out = combine_rmsnorm(xr, *pending[1:], final_norm, final=True)
    return out.reshape(batch, seq, d).astype(x.dtype)


def _final_rms(x, g, *, tm=512):
    m, d = x.shape
    return pl.pallas_call(
        _rms_kernel, grid=(m // tm,),
        in_specs=[_row_spec(tm, d), _vec_spec(d)],
        out_specs=_row_spec(tm, d),
        out_shape=jax.ShapeDtypeStruct((m, d), F32),
        compiler_params=_cparams("parallel"), name="final_rmsnorm",
    )(x, g.reshape(1, d))
```

```python
import functools

import jax
import jax.numpy as jnp
from jax import lax
from jax.experimental import pallas as pl
from jax.experimental.pallas import tpu as pltpu

BF16 = jnp.bfloat16
F32 = jnp.float32

D_MODEL = 2048
HEAD_DIM = 128
GDN_QK_HEADS = D_MODEL // HEAD_DIM
GDN_V_HEADS = 2 * GDN_QK_HEADS
GDN_QK_DIM = GDN_QK_HEADS * HEAD_DIM
GDN_V_DIM = GDN_V_HEADS * HEAD_DIM
GDN_CONV_DIM = 2 * GDN_QK_DIM + GDN_V_DIM
GDN_CONV = 4
GDN_CHUNK = 64
SGU_CHUNK = 128
SGU_INNER = 2 * D_MODEL
SGU_GROUPS = 16
SGU_GROUP_DIM = SGU_INNER // SGU_GROUPS
CONV_WIDTH = 31
N_EXPERTS = 8
N_MIXERS = 3
EPS = 1e-6
LN_EPS = 1e-5

LANES = 128
ROW_CHUNKS = D_MODEL // LANES
VMEM_LIMIT = 56 * 1024 * 1024
NEG_INF = float("-inf")


def _cparams(*sem):
    return pltpu.CompilerParams(dimension_semantics=sem, vmem_limit_bytes=VMEM_LIMIT)


def _silu(x):
    return x * jax.nn.sigmoid(x)


def _rms(x, g):
    return x * lax.rsqrt(jnp.mean(x * x, axis=-1, keepdims=True) + EPS) * g


def _rms_kernel(x_ref, g_ref, h_ref):
    h_ref[...] = _rms(x_ref[...], g_ref[...]).astype(h_ref.dtype)


def _add_rms_kernel(x_ref, d_ref, g_ref, xo_ref, h_ref):
    x = x_ref[...] + d_ref[...]
    xo_ref[...] = x
    h_ref[...] = _rms(x, g_ref[...]).astype(h_ref.dtype)


def _chunk_rows(c, tm):
    return pl.ds(c, tm, stride=ROW_CHUNKS)


def _token_row_dma(src_hbm, tok, dst_ref, r, sem):
    s = pl.multiple_of(tok * ROW_CHUNKS, ROW_CHUNKS)
    return pltpu.make_async_copy(src_hbm.at[pl.ds(s, ROW_CHUNKS)], dst_ref.at[pl.ds(r * ROW_CHUNKS, ROW_CHUNKS)], sem)


def _combine(x_ref, pos1_ref, pos2_ref, ys_hbm, w_ref, obuf_ref, sem):
    tm = x_ref.shape[0]

    def copies(r):
        return (_token_row_dma(ys_hbm, pos1_ref[r], obuf_ref.at[0], r, sem),
                _token_row_dma(ys_hbm, pos2_ref[r], obuf_ref.at[1], r, sem))

    def start(r, carry):
        for cp in copies(r):
            cp.start()
        return carry

    def wait(r, carry):
        for cp in copies(r):
            cp.wait()
        return carry

    lax.fori_loop(0, tm, start, 0, unroll=8)
    lax.fori_loop(0, tm, wait, 0, unroll=8)
    w = w_ref[...]
    w1, w2 = w[:, 0:1], w[:, 1:2]
    parts = []
    for c in range(ROW_CHUNKS):
        rows = _chunk_rows(c, tm)
        parts.append(x_ref[:, c * LANES:(c + 1) * LANES] + (w1 * obuf_ref[0, rows, :] + w2 * obuf_ref[1, rows, :]))
    return jnp.concatenate(parts, axis=1)


def _combine_rms_kernel(pos1_ref, pos2_ref, x_ref, ys_hbm, w_ref, g_ref, xo_ref, h_ref, obuf_ref, sem):
    x = _combine(x_ref, pos1_ref, pos2_ref, ys_hbm, w_ref, obuf_ref, sem)
    xo_ref[...] = x
    h_ref[...] = _rms(x, g_ref[...]).astype(h_ref.dtype)


def _combine_final_kernel(pos1_ref, pos2_ref, x_ref, ys_hbm, w_ref, g_ref, h_ref, obuf_ref, sem):
    x = _combine(x_ref, pos1_ref, pos2_ref, ys_hbm, w_ref, obuf_ref, sem)
    h_ref[...] = _rms(x, g_ref[...]).astype(h_ref.dtype)


def _row_spec(tm, d):
    return pl.BlockSpec((tm, d), lambda i: (i, 0))


def _vec_spec(d):
    return pl.BlockSpec((1, d), lambda i: (0, 0))


def rmsnorm(x, g, *, tm=512):
    m, d = x.shape
    return pl.pallas_call(
        _rms_kernel, grid=(m // tm,),
        in_specs=[_row_spec(tm, d), _vec_spec(d)],
        out_specs=_row_spec(tm, d),
        out_shape=jax.ShapeDtypeStruct((m, d), BF16),
        compiler_params=_cparams("parallel"), name="rmsnorm",
    )(x, g.reshape(1, d))


def add_rmsnorm(x, delta, g, *, tm=512):
    m, d = x.shape
    return pl.pallas_call(
        _add_rms_kernel, grid=(m // tm,),
        in_specs=[_row_spec(tm, d), _row_spec(tm, d), _vec_spec(d)],
        out_specs=[_row_spec(tm, d), _row_spec(tm, d)],
        out_shape=[jax.ShapeDtypeStruct((m, d), F32), jax.ShapeDtypeStruct((m, d), BF16)],
        compiler_params=_cparams("parallel"), name="add_rmsnorm",
    )(x, delta, g.reshape(1, d))


def combine_rmsnorm(x, ys, pos, wts, g, *, final, tm=256):
    m, d = x.shape
    nt = m // tm
    in_specs = [pl.BlockSpec((tm,), lambda i: (i,), memory_space=pltpu.SMEM),
                pl.BlockSpec((tm,), lambda i: (i + nt,), memory_space=pltpu.SMEM),
                _row_spec(tm, d), pl.BlockSpec(memory_space=pl.ANY), _row_spec(tm, LANES), _vec_spec(d)]
    scratch = [pltpu.VMEM((2, tm * ROW_CHUNKS, LANES), F32), pltpu.SemaphoreType.DMA(())]
    args = (pos, pos, x, ys, wts, g.reshape(1, d))
    if final:
        return pl.pallas_call(
            _combine_final_kernel, grid=(nt,), in_specs=in_specs,
            out_specs=_row_spec(tm, d), out_shape=jax.ShapeDtypeStruct((m, d), F32),
            scratch_shapes=scratch, compiler_params=_cparams("arbitrary"), name="combine_final_norm",
        )(*args)
    return pl.pallas_call(
        _combine_rms_kernel, grid=(nt,), in_specs=in_specs,
        out_specs=[_row_spec(tm, d), _row_spec(tm, d)],
        out_shape=[jax.ShapeDtypeStruct((m, d), F32), jax.ShapeDtypeStruct((m, d), BF16)],
        scratch_shapes=scratch, compiler_params=_cparams("arbitrary"), name="combine_rmsnorm",
    )(*args)


def _gelu_exact(x):
    return 0.5 * x * (1.0 + lax.erf(x * (2.0 ** -0.5)))


def _mm_kernel(*refs, nk, act, has_bias, has_res):
    x_ref, w_ref = refs[0], refs[1]
    pos = 2
    b_ref = r_ref = None
    if has_bias:
        b_ref = refs[pos]
        pos += 1
    if has_res:
        r_ref = refs[pos]
        pos += 1
    o_ref = refs[pos]
    acc_ref = refs[pos + 1] if nk > 1 else None

    def finalize(acc):
        if has_bias:
            acc = acc + b_ref[...]
        if act == "gelu":
            acc = _gelu_exact(acc)
        if has_res:
            acc = acc + r_ref[...]
        o_ref[...] = acc.astype(o_ref.dtype)

    part = jnp.dot(x_ref[...], w_ref[...], preferred_element_type=F32)
    if nk == 1:
        finalize(part)
        return
    k = pl.program_id(2)

    @pl.when(k == 0)
    def _():
        acc_ref[...] = part

    @pl.when(jnp.logical_and(k > 0, k < nk - 1))
    def _():
        acc_ref[...] += part

    @pl.when(k == nk - 1)
    def _():
        finalize(acc_ref[...] + part)


def matmul(x, w, *, bias=None, res=None, act=None, out_dtype=F32, tm=1024, tn=512, tk=2048):
    m, kd = x.shape
    n = w.shape[1]
    tm, tn, tk = min(tm, m), min(tn, n), min(tk, kd)
    nk = kd // tk
    in_specs = [pl.BlockSpec((tm, tk), lambda i, j, k: (i, k)),
                pl.BlockSpec((tk, tn), lambda i, j, k: (k, j))]
    args = [x, w]
    if bias is not None:
        in_specs.append(pl.BlockSpec((1, tn), lambda i, j, k: (0, j)))
        args.append(bias.reshape(1, n).astype(F32))
    if res is not None:
        in_specs.append(pl.BlockSpec((tm, tn), lambda i, j, k: (i, j)))
        args.append(res)
    return pl.pallas_call(
        functools.partial(_mm_kernel, nk=nk, act=act, has_bias=bias is not None, has_res=res is not None),
        grid=(m // tm, n // tn, nk),
        in_specs=in_specs,
        out_specs=pl.BlockSpec((tm, tn), lambda i, j, k: (i, j)),
        out_shape=jax.ShapeDtypeStruct((m, n), out_dtype),
        scratch_shapes=[pltpu.VMEM((tm, tn), F32)] if nk > 1 else [],
        compiler_params=_cparams("parallel", "parallel", "arbitrary"), name="matmul",
    )(*args)


def _glu_kernel(x_ref, wv_ref, wg_ref, bv_ref, bg_ref, o_ref):
    x = x_ref[...]
    val = jnp.dot(x, wv_ref[...], preferred_element_type=F32) + bv_ref[...]
    gate = jnp.dot(x, wg_ref[...], preferred_element_type=F32) + bg_ref[...]
    o_ref[...] = val * jax.nn.sigmoid(gate)


def glu_matmul(x, w, b, *, tm=1024, tn=512):
    m, kd = x.shape
    n = w.shape[1] // 2
    nb = n // tn
    b2 = b.reshape(1, 2 * n).astype(F32)
    return pl.pallas_call(
        _glu_kernel, grid=(m // tm, nb),
        in_specs=[pl.BlockSpec((tm, kd), lambda i, j: (i, 0)),
                  pl.BlockSpec((kd, tn), lambda i, j: (0, j)),
                  pl.BlockSpec((kd, tn), lambda i, j: (0, j + nb)),
                  pl.BlockSpec((1, tn), lambda i, j: (0, j)),
                  pl.BlockSpec((1, tn), lambda i, j: (0, j + nb))],
        out_specs=pl.BlockSpec((tm, tn), lambda i, j: (i, j)),
        out_shape=jax.ShapeDtypeStruct((m, n), F32),
        compiler_params=_cparams("parallel", "parallel"), name="glu_matmul",
    )(x, w, w, b2, b2)


def _swiglu_part(x, wg_ref, wu_ref, wd_ref):
    g = jnp.dot(x, wg_ref[0], preferred_element_type=F32)
    u = jnp.dot(x, wu_ref[0], preferred_element_type=F32)
    a = (_silu(g) * u).astype(BF16)
    return jnp.dot(a, wd_ref[0], preferred_element_type=F32)


def _ffn_kernel(te_ref, tv_ref, x_ref, wg_ref, wu_ref, wd_ref, o_ref):
    f = pl.program_id(1)
    part = _swiglu_part(x_ref[...], wg_ref, wu_ref, wd_ref)

    @pl.when(f == 0)
    def _():
        o_ref[...] = part

    @pl.when(f > 0)
    def _():
        o_ref[...] += part


GATHER_ROWS_PER_STEP = 64


def _ffn_gather_kernel(te_ref, tv_ref, cur_ref, nxt_ref, h_hbm, wg_ref, wu_ref, wd_ref, o_ref,
                       xbuf_ref, xb_ref, acc_ref, sems, *, tm, nf, nt):
    i = pl.program_id(0)
    f = pl.program_id(1)
    slot = i % 2

    @pl.when(jnp.logical_and(i == 0, f == 0))
    def _():
        def start(r, carry):
            _token_row_dma(h_hbm, cur_ref[r], xbuf_ref.at[0], r, sems.at[0]).start()
            return carry
        lax.fori_loop(0, tm, start, 0, unroll=8)

    @pl.when(jnp.logical_and(i + 1 < nt, f < tm // GATHER_ROWS_PER_STEP))
    def _():
        for j in range(GATHER_ROWS_PER_STEP):
            r = f * GATHER_ROWS_PER_STEP + j
            _token_row_dma(h_hbm, nxt_ref[r], xbuf_ref.at[1 - slot], r, sems.at[1 - slot]).start()

    @pl.when(f == 0)
    def _():
        def wait(r, carry):
            _token_row_dma(h_hbm, cur_ref[r], xbuf_ref.at[slot], r, sems.at[slot]).wait()
            return carry
        lax.fori_loop(0, tm, wait, 0, unroll=8)

    @pl.when(jnp.logical_and(tv_ref[i] == 0, f == 0))
    def _():
        o_ref[...] = jnp.zeros_like(o_ref)

    @pl.when(tv_ref[i] != 0)
    def _():
        @pl.when(f == 0)
        def _():
            for c in range(ROW_CHUNKS):
                xb_ref[:, c * LANES:(c + 1) * LANES] = xbuf_ref[slot, _chunk_rows(c, tm), :].astype(BF16)

        part = _swiglu_part(xb_ref[...], wg_ref, wu_ref, wd_ref)

        @pl.when(f == 0)
        def _():
            acc_ref[...] = part

        @pl.when(jnp.logical_and(f > 0, f < nf - 1))
        def _():
            acc_ref[...] += part

        @pl.when(f == nf - 1)
        def _():
            for c in range(ROW_CHUNKS):
                cols = slice(c * LANES, (c + 1) * LANES)
                o_ref[_chunk_rows(c, tm), :] = acc_ref[:, cols] + part[:, cols]


def _ffn_weight_specs(d, tf, nf):
    def wcol(i, f, te, tv):
        return (te[i], 0, jnp.where(tv[i] != 0, f, nf - 1))

    def wrow(i, f, te, tv):
        return (te[i], jnp.where(tv[i] != 0, f, nf - 1), 0)

    return [pl.BlockSpec((1, d, tf), wcol), pl.BlockSpec((1, d, tf), wcol), pl.BlockSpec((1, tf, d), wrow)]


def grouped_ffn(x, w_gate, w_up, w_down, tile_expert, tile_valid, *, tm, tf=512):
    mp, d = x.shape
    nf = w_gate.shape[2] // tf
    row_block = pl.BlockSpec((tm, d), lambda i, f, te, tv: (i, 0))
    grid_spec = pltpu.PrefetchScalarGridSpec(
        num_scalar_prefetch=2, grid=(mp // tm, nf),
        in_specs=[row_block] + _ffn_weight_specs(d, tf, nf), out_specs=row_block)
    return pl.pallas_call(
        _ffn_kernel, grid_spec=grid_spec, out_shape=jax.ShapeDtypeStruct((mp, d), F32),
        compiler_params=_cparams("parallel", "arbitrary"), name="grouped_ffn",
    )(tile_expert, tile_valid, x, w_gate, w_up, w_down)


def gathered_ffn(h, row_token, w_gate, w_up, w_down, tile_expert, tile_valid, *, tm, tf=512):
    d = w_gate.shape[1]
    mp = row_token.shape[0]
    nt = mp // tm
    nf = w_gate.shape[2] // tf
    assert tm % GATHER_ROWS_PER_STEP == 0 and tm // GATHER_ROWS_PER_STEP <= nf
    smem_rows = functools.partial(pl.BlockSpec, (tm,), memory_space=pltpu.SMEM)
    grid_spec = pltpu.PrefetchScalarGridSpec(
        num_scalar_prefetch=2, grid=(nt, nf),
        in_specs=[smem_rows(lambda i, f, te, tv: (i,)),
                  smem_rows(lambda i, f, te, tv: (jnp.minimum(i + 1, nt - 1),)),
                  pl.BlockSpec(memory_space=pl.ANY)] + _ffn_weight_specs(d, tf, nf),
        out_specs=pl.BlockSpec((tm * ROW_CHUNKS, LANES), lambda i, f, te, tv: (i, 0)),
        scratch_shapes=[pltpu.VMEM((2, tm * ROW_CHUNKS, LANES), F32), pltpu.VMEM((tm, d), BF16),
                        pltpu.VMEM((tm, d), F32), pltpu.SemaphoreType.DMA((2,))])
    return pl.pallas_call(
        functools.partial(_ffn_gather_kernel, tm=tm, nf=nf, nt=nt), grid_spec=grid_spec,
        out_shape=jax.ShapeDtypeStruct((mp * ROW_CHUNKS, LANES), F32),
        compiler_params=_cparams("arbitrary", "arbitrary"), name="gathered_ffn",
    )(tile_expert, tile_valid, row_token, row_token, h, w_gate, w_up, w_down)


def _split_bf16(x):
    hi = x.astype(BF16)
    lo = (x - hi.astype(F32)).astype(BF16)
    return hi, lo


def _router_kernel(x_ref, g_ref, rhi_ref, rlo_ref, h_ref, mi_ref, mf_ref, cnt_ref, carry_ref, *, tm):
    i = pl.program_id(0)

    @pl.when(i == 0)
    def _():
        carry_ref[...] = jnp.zeros_like(carry_ref)

    h = _rms(x_ref[...], g_ref[...])
    for c in range(ROW_CHUNKS):
        h_ref[_chunk_rows(c, tm), :] = h[:, c * LANES:(c + 1) * LANES]
    h_hi, h_lo = _split_bf16(h)
    logits = (jnp.dot(h_hi, rhi_ref[...], preferred_element_type=F32)
              + (jnp.dot(h_lo, rhi_ref[...], preferred_element_type=F32)
                 + jnp.dot(h_hi, rlo_ref[...], preferred_element_type=F32)))
    lane = lax.broadcasted_iota(jnp.int32, (tm, LANES), 1)
    lg = jnp.where(lane < N_EXPERTS, logits, NEG_INF)
    m1 = jnp.max(lg, axis=1, keepdims=True)
    i1 = jnp.min(jnp.where(lg == m1, lane, LANES), axis=1, keepdims=True)
    lg2 = jnp.where(lane == i1, NEG_INF, lg)
    m2 = jnp.max(lg2, axis=1, keepdims=True)
    i2 = jnp.min(jnp.where(lg2 == m2, lane, LANES), axis=1, keepdims=True)
    e2 = jnp.exp(m2 - m1)
    w1 = 1.0 / (1.0 + e2)
    w2 = e2 / (1.0 + e2)
    oh1 = (lane == i1).astype(F32)
    oh2 = (lane == i2).astype(F32)
    tot = oh1 + oh2
    r = lax.broadcasted_iota(jnp.int32, (tm, tm), 0)
    c = lax.broadcasted_iota(jnp.int32, (tm, tm), 1)
    before = jnp.dot((c < r).astype(BF16), tot.astype(BF16), preferred_element_type=F32)
    base = before + carry_ref[0:1, :]
    rank1 = jnp.sum(oh1 * base, axis=1, keepdims=True).astype(jnp.int32)
    rank2 = jnp.sum(oh2 * base, axis=1, keepdims=True).astype(jnp.int32)
    mi_ref[...] = jnp.where(lane == 0, i1, jnp.where(lane == 1, i2, jnp.where(lane == 2, rank1, rank2)))
    mf_ref[...] = jnp.where(lane == 0, w1, w2)
    carry = carry_ref[...] + jnp.sum(tot, axis=0, keepdims=True)
    carry_ref[...] = carry
    cnt_ref[...] = carry


def route(x, g, router, *, tm=512):
    m, d = x.shape
    rpad = jnp.zeros((d, LANES), F32).at[:, :N_EXPERTS].set(router.astype(F32))
    rhi, rlo = _split_bf16(rpad)
    return pl.pallas_call(
        functools.partial(_router_kernel, tm=tm), grid=(m // tm,),
        in_specs=[_row_spec(tm, d), _vec_spec(d),
                  pl.BlockSpec((d, LANES), lambda i: (0, 0)), pl.BlockSpec((d, LANES), lambda i: (0, 0))],
        out_specs=[_row_spec(tm * ROW_CHUNKS, LANES), _row_spec(tm, LANES), _row_spec(tm, LANES),
                   pl.BlockSpec((8, LANES), lambda i: (0, 0))],
        out_shape=[jax.ShapeDtypeStruct((m * ROW_CHUNKS, LANES), F32), jax.ShapeDtypeStruct((m, LANES), jnp.int32),
                   jax.ShapeDtypeStruct((m, LANES), F32), jax.ShapeDtypeStruct((8, LANES), F32)],
        scratch_shapes=[pltpu.VMEM((8, LANES), F32)],
        compiler_params=_cparams("arbitrary"), name="moe_route",
    )(x, g.reshape(1, d), rhi, rlo)


def moe_layer(x, g, router, w_gate, w_up, w_down, *, expert_base=0, tm=512):
    n, d = x.shape
    h, meta_i, wts, counts = route(x, g, router)
    e1, e2, r1, r2 = (meta_i[:, k] for k in range(4))
    cnt = counts[0, :N_EXPERTS].astype(jnp.int32)
    padded = ((cnt + tm - 1) // tm) * tm
    ends = jnp.cumsum(padded)
    starts = ends - padded
    pos1 = starts[e1] + r1
    pos2 = starts[e2] + r2
    n_tiles = (2 * n) // tm + N_EXPERTS
    tile_start = jnp.arange(n_tiles, dtype=jnp.int32) * tm
    tile_expert = jnp.minimum(jnp.sum((tile_start[:, None] >= ends[None, :]).astype(jnp.int32), axis=1),
                              N_EXPERTS - 1)
    tile_valid = (tile_start < ends[-1]).astype(jnp.int32)
    tok = jnp.arange(n, dtype=jnp.int32)
    pos = jnp.concatenate([pos1, pos2])
    row_token = jnp.zeros((n_tiles * tm,), jnp.int32).at[pos].set(jnp.concatenate([tok, tok]))
    ys = gathered_ffn(h, row_token, w_gate, w_up, w_down, tile_expert + expert_base, tile_valid, tm=tm)
    return ys, pos, wts


def _gdn_gates_kernel(ba_ref, alog_ref, dt_ref, o_ref, *, tm):
    ba = ba_ref[...]
    sp = ba + dt_ref[...]
    softplus = jnp.maximum(sp, 0.0) + jnp.log1p(jnp.exp(-jnp.abs(sp)))
    gcum = -jnp.exp(alog_ref[...]) * softplus
    inchunk = lax.broadcasted_iota(jnp.int32, (tm, LANES), 0) % GDN_CHUNK
    sh = 1
    while sh < GDN_CHUNK:
        gcum = gcum + jnp.where(inchunk >= sh, pltpu.roll(gcum, sh, 0), 0.0)
        sh *= 2
    lane = lax.broadcasted_iota(jnp.int32, (tm, LANES), 1)
    o_ref[...] = jnp.where(lane < GDN_V_HEADS, jax.nn.sigmoid(ba), gcum)


def gdn_gates(ba, a_log, dt_bias, *, tm=512):
    m = ba.shape[0]
    alog_row = jnp.zeros((1, LANES), F32).at[0, GDN_V_HEADS:2 * GDN_V_HEADS].set(a_log.astype(F32))
    dt_row = jnp.zeros((1, LANES), F32).at[0, GDN_V_HEADS:2 * GDN_V_HEADS].set(dt_bias.astype(F32))
    return pl.pallas_call(
        functools.partial(_gdn_gates_kernel, tm=tm), grid=(m // tm,),
        in_specs=[_row_spec(tm, LANES), _vec_spec(LANES), _vec_spec(LANES)],
        out_specs=_row_spec(tm, LANES), out_shape=jax.ShapeDtypeStruct((m, LANES), F32),
        compiler_params=_cparams("parallel"), name="gdn_gates",
    )(ba, alog_row, dt_row)


def _gdn_kernel(q_ref, k_ref, v_ref, z_ref, gates_ref, cwq_ref, cwk_ref, cwv_ref, nw_ref,
                o_ref, state_ref, hq_ref, hk_ref, hv_ref, xq_ref, xk_ref, xv_ref, *, ts):
    p = pl.program_id(1)
    s = pl.program_id(2)
    hd = HEAD_DIM

    @pl.when(s == 0)
    def _():
        state_ref[...] = jnp.zeros_like(state_ref)
        hq_ref[...] = jnp.zeros_like(hq_ref)
        hk_ref[...] = jnp.zeros_like(hk_ref)
        hv_ref[...] = jnp.zeros_like(hv_ref)

    def conv_silu(x_ref, hist_ref, xx_ref, w_ref):
        xx_ref[0:8, :] = hist_ref[...]
        xx_ref[8:, :] = x_ref[...]
        hist_ref[...] = x_ref[ts - 8:, :]
        w = w_ref[...]
        acc = w[0:1, :] * xx_ref[5:5 + ts, :]
        for j in range(1, GDN_CONV):
            acc = acc + w[j:j + 1, :] * xx_ref[5 + j:5 + j + ts, :]
        return _silu(acc)

    def l2n(x):
        return x * lax.rsqrt(jnp.sum(x * x, axis=-1, keepdims=True) + EPS)

    q = l2n(conv_silu(q_ref, hq_ref, xq_ref, cwq_ref)) * (hd ** -0.5)
    k = l2n(conv_silu(k_ref, hk_ref, xk_ref, cwk_ref))
    v = conv_silu(v_ref, hv_ref, xv_ref, cwv_ref)

    gates = gates_ref[...]
    lane = lax.broadcasted_iota(jnp.int32, (ts, LANES), 1)

    def pick(idx):
        return jnp.sum(jnp.where(lane == idx, gates, 0.0), axis=1, keepdims=True)

    beta = [pick(2 * p + h) for h in range(2)]
    gc = [pick(GDN_V_HEADS + 2 * p + h) for h in range(2)]
    egc = [jnp.exp(x) for x in gc]

    sc = 2 * GDN_CHUNK
    r128 = lax.broadcasted_iota(jnp.int32, (sc, sc), 0)
    c128 = lax.broadcasted_iota(jnp.int32, (sc, sc), 1)
    same_chunk = (r128 // GDN_CHUNK) == (c128 // GDN_CHUNK)
    incl = jnp.logical_and(same_chunk, c128 <= r128)
    strict = jnp.logical_and(same_chunk, c128 < r128)

    def mm(a, b):
        return jnp.dot(a.astype(BF16), b.astype(BF16), preferred_element_type=F32)

    qb = q.astype(BF16)
    kb = k.astype(BF16)
    n_blk = ts // sc
    chains = [(u, h) for u in range(n_blk) for h in range(2)]
    blk_rows = [slice(u * sc, (u + 1) * sc) for u in range(n_blk)]
    ab = [lax.dot_general(jnp.concatenate([qb[r], kb[r]], axis=0), kb[r], (((1,), (1,)), ((), ())),
                          preferred_element_type=F32) for r in blk_rows]
    kt = [k[r].T.astype(BF16) for r in blk_rows]
    eye = (r128 == c128).astype(F32)
    same16 = (r128 // 16) == (c128 // 16)
    same32 = (r128 // 32) == (c128 // 32)
    lmat, attn, ymat = [], [], []
    for u, h in chains:
        r = blk_rows[u]
        gcol, bcol, ecol = gc[h][r], beta[h][r], egc[h][r]
        grow = jnp.broadcast_to(gcol, (sc, sc)).T
        decay = jnp.exp(jnp.where(incl, gcol - grow, NEG_INF))
        lmat.append(jnp.where(strict, ab[u][sc:] * bcol * decay, 0.0))
        attn.append(ab[u][:sc] * decay)
        ymat.append(jnp.concatenate([bcol * v[r, h * hd:(h + 1) * hd], (bcol * ecol) * k[r]], axis=1))
    nd = [jnp.where(same16, -m, 0.0) for m in lmat]
    x = [eye + m for m in nd]
    pw = [mm(m, m) for m in nd]
    for level in range(3):
        xp = [mm(a, b) for a, b in zip(x, pw)]
        if level < 2:
            pw = [mm(m, m) for m in pw]
        x = [a + b for a, b in zip(x, xp)]
    for inner, outer in ((same16, same32), (same32, same_chunk)):
        off = [jnp.where(jnp.logical_and(outer, jnp.logical_not(inner)), m, 0.0) for m in lmat]
        t1 = [mm(a, b) for a, b in zip(x, off)]
        t2 = [mm(a, b) for a, b in zip(t1, x)]
        x = [a - b for a, b in zip(x, t2)]
    uw = [mm(a, b) for a, b in zip(x, ymat)]
    dw = [mm(a, b) for a, b in zip(attn, uw)]
    steps = [[] for _ in range(2)]
    for i, (u, h) in enumerate(chains):
        r = blk_rows[u]
        cmat = egc[h][r] * q[r] - dw[i][:, hd:]
        for c in range(2):
            lrow = slice(c * GDN_CHUNK, (c + 1) * GDN_CHUNK)
            last = u * sc + (c + 1) * GDN_CHUNK - 1
            gl = gc[h][last:last + 1]
            tail = jnp.exp(gl - gc[h][r][lrow]) * uw[i][lrow]
            zeros = jnp.zeros_like(tail)
            full = jnp.concatenate([tail, zeros] if c == 0 else [zeros, tail], axis=0)
            bm = jnp.dot(kt[u], full.astype(BF16), preferred_element_type=F32)
            mc = jnp.concatenate([bm[:, hd:], cmat[lrow]], axis=0).astype(BF16)
            steps[h].append((mc, bm[:, :hd], dw[i][lrow, :hd], jnp.exp(gl)))

    states = [state_ref[h] for h in range(2)]
    outs = [[] for _ in range(2)]
    for cc in range(ts // GDN_CHUNK):
        for h in range(2):
            mc, bmat, dmat, glast = steps[h][cc]
            ms = jnp.dot(mc, states[h].astype(BF16), preferred_element_type=F32)
            outs[h].append(ms[hd:] + dmat)
            states[h] = glast * states[h] + (bmat - ms[:hd])
    for h in range(2):
        state_ref[h] = states[h]
    o = jnp.concatenate([jnp.concatenate(outs[h], axis=0) for h in range(2)], axis=1)

    z = z_ref[...]
    nw = nw_ref[...]
    ys = []
    for h in range(2):
        hs = slice(h * hd, (h + 1) * hd)
        ys.append(_rms(o[:, hs], nw) * _silu(z[:, hs]))
    o_ref[...] = jnp.concatenate(ys, axis=1).astype(o_ref.dtype)


def gdn_core(proj, gates, conv_w, norm_w, *, batch, seq, ts=512):
    hd = HEAD_DIM
    ns = seq // ts
    n_pairs = GDN_QK_HEADS
    kq = GDN_QK_DIM // hd
    vb = (2 * GDN_QK_DIM) // (2 * hd)
    zb = GDN_CONV_DIM // (2 * hd)

    def rows(b, p, s):
        return b * ns + s

    in_specs = [
        pl.BlockSpec((ts, hd), lambda b, p, s: (rows(b, p, s), p)),
        pl.BlockSpec((ts, hd), lambda b, p, s: (rows(b, p, s), kq + p)),
        pl.BlockSpec((ts, 2 * hd), lambda b, p, s: (rows(b, p, s), vb + p)),
        pl.BlockSpec((ts, 2 * hd), lambda b, p, s: (rows(b, p, s), zb + p)),
        pl.BlockSpec((ts, LANES), lambda b, p, s: (rows(b, p, s), 0)),
        pl.BlockSpec((GDN_CONV, hd), lambda b, p, s: (0, p)),
        pl.BlockSpec((GDN_CONV, hd), lambda b, p, s: (0, kq + p)),
        pl.BlockSpec((GDN_CONV, 2 * hd), lambda b, p, s: (0, vb + p)),
        pl.BlockSpec((1, hd), lambda b, p, s: (0, 0)),
    ]
    return pl.pallas_call(
        functools.partial(_gdn_kernel, ts=ts), grid=(batch, n_pairs, ns),
        in_specs=in_specs,
        out_specs=pl.BlockSpec((ts, 2 * hd), lambda b, p, s: (rows(b, p, s), p)),
        out_shape=jax.ShapeDtypeStruct((batch * seq, GDN_V_DIM), BF16),
        scratch_shapes=[pltpu.VMEM((2, hd, hd), F32),
                        pltpu.VMEM((8, hd), F32), pltpu.VMEM((8, hd), F32), pltpu.VMEM((8, 2 * hd), F32),
                        pltpu.VMEM((ts + 8, hd), F32), pltpu.VMEM((ts + 8, hd), F32),
                        pltpu.VMEM((ts + 8, 2 * hd), F32)],
        compiler_params=_cparams("parallel", "parallel", "arbitrary"), name="gdn_core",
    )(proj, proj, proj, proj, gates, conv_w, conv_w, conv_w, norm_w.reshape(1, hd).astype(F32))


def gated_deltanet(x, h, w_in, conv_w, a_log, dt_bias, norm_w, w_out, *, batch, seq):
    main = GDN_CONV_DIM + GDN_V_DIM
    w_main = w_in[:, :main].astype(BF16)
    w_ba = jnp.zeros((w_in.shape[0], LANES), F32).at[:, :2 * GDN_V_HEADS].set(w_in[:, main:]).astype(BF16)
    proj = matmul(h, w_main)
    gates = gdn_gates(matmul(h, w_ba), a_log, dt_bias)
    y = gdn_core(proj, gates, conv_w.astype(F32), norm_w, batch=batch, seq=seq)
    return matmul(y, w_out.astype(BF16), res=x)


def _sgu_kernel(u_ref, v_ref, g_ref, b_ref, ws_ref, bs_ref, o_ref, *, nch):
    r = lax.broadcasted_iota(jnp.int32, (SGU_CHUNK, SGU_CHUNK), 0)
    c = lax.broadcasted_iota(jnp.int32, (SGU_CHUNK, SGU_CHUNK), 1)
    causal = c <= r
    gd = SGU_GROUP_DIM
    for ch in range(nch):
        rows = slice(ch * SGU_CHUNK, (ch + 1) * SGU_CHUNK)
        v = v_ref[rows, :]
        mu = jnp.mean(v, axis=-1, keepdims=True)
        vc = v - mu
        vn = vc * lax.rsqrt(jnp.mean(vc * vc, axis=-1, keepdims=True) + LN_EPS)
        vn = (vn * g_ref[...] + b_ref[...]).astype(BF16)
        for grp in range(SGU_GROUPS):
            cols = slice(grp * gd, (grp + 1) * gd)
            w = jnp.where(causal, ws_ref[grp], 0.0).astype(BF16)
            sv = jnp.dot(w, vn[:, cols], preferred_element_type=F32) + bs_ref[:, grp:grp + 1]
            o_ref[rows, cols] = (u_ref[rows, cols] * sv).astype(o_ref.dtype)


def sgu_spatial(p, ln_g, ln_b, w_s, b_s, *, nch=2):
    m = p.shape[0]
    rows = nch * SGU_CHUNK
    inner = SGU_INNER
    return pl.pallas_call(
        functools.partial(_sgu_kernel, nch=nch), grid=(m // rows,),
        in_specs=[pl.BlockSpec((rows, inner), lambda i: (i, 0)),
                  pl.BlockSpec((rows, inner), lambda i: (i, 1)),
                  _vec_spec(inner), _vec_spec(inner),
                  pl.BlockSpec((SGU_GROUPS, SGU_CHUNK, SGU_CHUNK), lambda i: (0, 0, 0)),
                  pl.BlockSpec((SGU_CHUNK, SGU_GROUPS), lambda i: (0, 0))],
        out_specs=pl.BlockSpec((rows, inner), lambda i: (i, 0)),
        out_shape=jax.ShapeDtypeStruct((m, inner), BF16),
        compiler_params=_cparams("parallel"), name="sgu_spatial",
    )(p, p, ln_g.reshape(1, inner).astype(F32), ln_b.reshape(1, inner).astype(F32),
      w_s.astype(F32), b_s.T.astype(F32))


def chunked_sgu(x, h, w_in, b_in, ln_g, ln_b, w_s, b_s, w_out, b_out):
    p = matmul(h, w_in.astype(BF16), bias=b_in, act="gelu")
    uv = sgu_spatial(p, ln_g, ln_b, w_s, b_s)
    return matmul(uv, w_out.astype(BF16), bias=b_out, res=x)


CV_HALO = 32


def _cvconv_kernel(halo_ref, y_ref, w_ref, bdw_ref, g_ref, b_ref, o_ref, xx_ref, yc_ref, *, ts, cb):
    s = pl.program_id(1)
    d = y_ref.shape[1]

    @pl.when(s == 0)
    def _():
        xx_ref[0:CV_HALO, :] = jnp.zeros((CV_HALO, d), F32)

    @pl.when(s > 0)
    def _():
        xx_ref[0:CV_HALO, :] = halo_ref[...]

    xx_ref[CV_HALO:, :] = y_ref[...]
    off = CV_HALO - (CONV_WIDTH - 1)
    for blk in range(d // cb):
        cols = slice(blk * cb, (blk + 1) * cb)
        w = w_ref[:, cols]
        acc = w[0:1, :] * xx_ref[off:off + ts, cols]
        for j in range(1, CONV_WIDTH):
            acc = acc + w[j:j + 1, :] * xx_ref[off + j:off + j + ts, cols]
        yc_ref[:, cols] = acc + bdw_ref[:, cols]
    y = yc_ref[...]
    mu = jnp.mean(y, axis=-1, keepdims=True)
    yc = y - mu
    yn = yc * lax.rsqrt(jnp.mean(yc * yc, axis=-1, keepdims=True) + LN_EPS)
    o_ref[...] = _silu(yn * g_ref[...] + b_ref[...]).astype(o_ref.dtype)


def conformer_dwconv(y, w_dw, b_dw, ln_g, ln_b, *, batch, seq, ts=128, cb=256):
    m, d = y.shape
    ns = seq // ts
    hb = ts // CV_HALO
    return pl.pallas_call(
        functools.partial(_cvconv_kernel, ts=ts, cb=cb), grid=(batch, ns),
        in_specs=[pl.BlockSpec((CV_HALO, d), lambda b, s: (jnp.maximum((b * ns + s) * hb - 1, 0), 0)),
                  pl.BlockSpec((ts, d), lambda b, s: (b * ns + s, 0)),
                  pl.BlockSpec((CONV_WIDTH, d), lambda b, s: (0, 0)),
                  pl.BlockSpec((1, d), lambda b, s: (0, 0)),
                  pl.BlockSpec((1, d), lambda b, s: (0, 0)),
                  pl.BlockSpec((1, d), lambda b, s: (0, 0))],
        out_specs=pl.BlockSpec((ts, d), lambda b, s: (b * ns + s, 0)),
        out_shape=jax.ShapeDtypeStruct((m, d), BF16),
        scratch_shapes=[pltpu.VMEM((ts + CV_HALO, d), F32), pltpu.VMEM((ts, d), F32)],
        compiler_params=_cparams("parallel", "parallel"), name="conformer_dwconv",
    )(y, y, w_dw.astype(F32), b_dw.reshape(1, d).astype(F32), ln_g.reshape(1, d).astype(F32),
      ln_b.reshape(1, d).astype(F32))


def conformer_conv(x, h, w_pw1, b_pw1, w_dw, b_dw, ln_g, ln_b, w_pw2, b_pw2, *, batch, seq):
    y = glu_matmul(h, w_pw1.astype(BF16), b_pw1)
    yn = conformer_dwconv(y, w_dw, b_dw, ln_g, ln_b, batch=batch, seq=seq)
    return matmul(yn, w_pw2.astype(BF16), bias=b_pw2, res=x)


def dense_ffn(h, w_gate, w_up, w_down, *, tm=1024):
    m = h.shape[0]
    nt = m // tm
    return grouped_ffn(h, w_gate[None].astype(BF16), w_up[None].astype(BF16), w_down[None].astype(BF16),
                       jnp.zeros((nt,), jnp.int32), jnp.ones((nt,), jnp.int32), tm=tm)


def kernel(x, mix_norm, ffn_norm, gdn_w_in, gdn_conv_w, gdn_a_log, gdn_dt_bias, gdn_norm, gdn_w_out, sgu_w_in, sgu_b_in, sgu_ln_g, sgu_ln_b, sgu_w_s, sgu_b_s, sgu_w_out, sgu_b_out, cv_w_pw1, cv_b_pw1, cv_w_dw, cv_b_dw, cv_ln_g, cv_ln_b, cv_w_pw2, cv_b_pw2, ffn_w_gate, ffn_w_up, ffn_w_down, moe_router, moe_w_gate, moe_w_up, moe_w_down, final_norm):
    batch, seq, d = x.shape
    depth = mix_norm.shape[0]
    xr = x.reshape(batch * seq, d).astype(F32)
    pending = None
    moe_wg, moe_wu, moe_wd = (w.astype(BF16).reshape((-1,) + w.shape[2:]) for w in (moe_w_gate, moe_w_up, moe_w_down))
    for i in range(depth):
        if pending is None:
            h = rmsnorm(xr, mix_norm[i])
        elif pending[0] == "dense":
            xr, h = add_rmsnorm(xr, pending[1], mix_norm[i])
        else:
            xr, h = combine_rmsnorm(xr, *pending[1:], mix_norm[i], final=False)
        kind, j = i % N_MIXERS, i // N_MIXERS
        if kind == 0:
            xr = gated_deltanet(xr, h, gdn_w_in[j], gdn_conv_w[j], gdn_a_log[j], gdn_dt_bias[j], gdn_norm[j],
                                gdn_w_out[j], batch=batch, seq=seq)
        elif kind == 1:
            xr = chunked_sgu(xr, h, sgu_w_in[j], sgu_b_in[j], sgu_ln_g[j], sgu_ln_b[j], sgu_w_s[j], sgu_b_s[j],
                             sgu_w_out[j], sgu_b_out[j])
        else:
            xr = conformer_conv(xr, h, cv_w_pw1[j], cv_b_pw1[j], cv_w_dw[j], cv_b_dw[j], cv_ln_g[j], cv_ln_b[j],
                                cv_w_pw2[j], cv_b_pw2[j], batch=batch, seq=seq)
        j = i // 2
        if i % 2 == 0:
            h = rmsnorm(xr, ffn_norm[i])
            pending = ("dense", dense_ffn(h, ffn_w_gate[j], ffn_w_up[j], ffn_w_down[j]))
        else:
            pending = ("moe",) + moe_layer(xr, ffn_norm[i], moe_router[j], moe_wg, moe_wu, moe_wd,
                                           expert_base=j * N_EXPERTS)
    if pending[0] == "dense":
        xr, _ = add_rmsnorm(xr, pending[1], final_norm)
        out = _final_rms(xr, final_norm)
    else:
        out = combine_rmsnorm(xr, *pending[1:], final_norm, final=True)
    return out.reshape(batch, seq, d).astype(x.dtype)


def _final_rms(x, g, *, tm=512):
    m, d = x.shape
    return pl.pallas_call(
        _rms_kernel, grid=(m // tm,),
        in_specs=[_row_spec(tm, d), _vec_spec(d)],
        out_specs=_row_spec(tm, d),
        out_shape=jax.ShapeDtypeStruct((m, d), F32),
        compiler_params=_cparams("parallel"), name="final_rmsnorm",
    )(x, g.reshape(1, d))
```

```python
import functools

import jax
import jax.numpy as jnp
from jax import lax
from jax.experimental import pallas as pl
from jax.experimental.pallas import tpu as pltpu

BF16 = jnp.bfloat16
F32 = jnp.float32

D_MODEL = 2048
HEAD_DIM = 128
GDN_QK_HEADS = D_MODEL // HEAD_DIM
GDN_V_HEADS = 2 * GDN_QK_HEADS
GDN_QK_DIM = GDN_QK_HEADS * HEAD_DIM
GDN_V_DIM = GDN_V_HEADS * HEAD_DIM
GDN_CONV_DIM = 2 * GDN_QK_DIM + GDN_V_DIM
GDN_CONV = 4
GDN_CHUNK = 64
SGU_CHUNK = 128
SGU_INNER = 2 * D_MODEL
SGU_GROUPS = 16
SGU_GROUP_DIM = SGU_INNER // SGU_GROUPS
CONV_WIDTH = 31
N_EXPERTS = 8
N_MIXERS = 3
EPS = 1e-6
LN_EPS = 1e-5

LANES = 128
ROW_CHUNKS = D_MODEL // LANES
VMEM_LIMIT = 56 * 1024 * 1024
NEG_INF = float("-inf")


def _cparams(*sem):
    return pltpu.CompilerParams(dimension_semantics=sem, vmem_limit_bytes=VMEM_LIMIT)


def _silu(x):
    return x * jax.nn.sigmoid(x)


def _rms(x, g):
    return x * lax.rsqrt(jnp.mean(x * x, axis=-1, keepdims=True) + EPS) * g


def _rms_kernel(x_ref, g_ref, h_ref):
    h_ref[...] = _rms(x_ref[...], g_ref[...]).astype(h_ref.dtype)


def _add_rms_kernel(x_ref, d_ref, g_ref, xo_ref, h_ref):
    x = x_ref[...] + d_ref[...]
    xo_ref[...] = x
    h_ref[...] = _rms(x, g_ref[...]).astype(h_ref.dtype)


def _chunk_rows(c, tm):
    return pl.ds(c, tm, stride=ROW_CHUNKS)


def _token_row_dma(src_hbm, tok, dst_ref, r, sem):
    s = pl.multiple_of(tok * ROW_CHUNKS, ROW_CHUNKS)
    return pltpu.make_async_copy(src_hbm.at[pl.ds(s, ROW_CHUNKS)], dst_ref.at[pl.ds(r * ROW_CHUNKS, ROW_CHUNKS)], sem)


def _combine(x_ref, pos1_ref, pos2_ref, ys_hbm, w_ref, obuf_ref, sem):
    tm = x_ref.shape[0]

    def copies(r):
        return (_token_row_dma(ys_hbm, pos1_ref[r], obuf_ref.at[0], r, sem),
                _token_row_dma(ys_hbm, pos2_ref[r], obuf_ref.at[1], r, sem))

    def start(r, carry):
        for cp in copies(r):
            cp.start()
        return carry

    def wait(r, carry):
        for cp in copies(r):
            cp.wait()
        return carry

    lax.fori_loop(0, tm, start, 0, unroll=8)
    lax.fori_loop(0, tm, wait, 0, unroll=8)
    w = w_ref[...]
    w1, w2 = w[:, 0:1], w[:, 1:2]
    parts = []
    for c in range(ROW_CHUNKS):
        rows = _chunk_rows(c, tm)
        parts.append(x_ref[:, c * LANES:(c + 1) * LANES] + (w1 * obuf_ref[0, rows, :] + w2 * obuf_ref[1, rows, :]))
    return jnp.concatenate(parts, axis=1)


def _combine_rms_kernel(pos1_ref, pos2_ref, x_ref, ys_hbm, w_ref, g_ref, xo_ref, h_ref, obuf_ref, sem):
    x = _combine(x_ref, pos1_ref, pos2_ref, ys_hbm, w_ref, obuf_ref, sem)
    xo_ref[...] = x
    h_ref[...] = _rms(x, g_ref[...]).astype(h_ref.dtype)


def _combine_final_kernel(pos1_ref, pos2_ref, x_ref, ys_hbm, w_ref, g_ref, h_ref, obuf_ref, sem):
    x = _combine(x_ref, pos1_ref, pos2_ref, ys_hbm, w_ref, obuf_ref, sem)
    h_ref[...] = _rms(x, g_ref[...]).astype(h_ref.dtype)


def _row_spec(tm, d):
    return pl.BlockSpec((tm, d), lambda i: (i, 0))


def _vec_spec(d):
    return pl.BlockSpec((1, d), lambda i: (0, 0))


def rmsnorm(x, g, *, tm=512):
    m, d = x.shape
    return pl.pallas_call(
        _rms_kernel, grid=(m // tm,),
        in_specs=[_row_spec(tm, d), _vec_spec(d)],
        out_specs=_row_spec(tm, d),
        out_shape=jax.ShapeDtypeStruct((m, d), BF16),
        compiler_params=_cparams("parallel"), name="rmsnorm",
    )(x, g.reshape(1, d))


def add_rmsnorm(x, delta, g, *, tm=512):
    m, d = x.shape
    return pl.pallas_call(
        _add_rms_kernel, grid=(m // tm,),
        in_specs=[_row_spec(tm, d), _row_spec(tm, d), _vec_spec(d)],
        out_specs=[_row_spec(tm, d), _row_spec(tm, d)],
        out_shape=[jax.ShapeDtypeStruct((m, d), F32), jax.ShapeDtypeStruct((m, d), BF16)],
        compiler_params=_cparams("parallel"), name="add_rmsnorm",
    )(x, delta, g.reshape(1, d))


def combine_rmsnorm(x, ys, pos, wts, g, *, final, tm=256):
    m, d = x.shape
    nt = m // tm
    in_specs = [pl.BlockSpec((tm,), lambda i: (i,), memory_space=pltpu.SMEM),
                pl.BlockSpec((tm,), lambda i: (i + nt,), memory_space=pltpu.SMEM),
                _row_spec(tm, d), pl.BlockSpec(memory_space=pl.ANY), _row_spec(tm, LANES), _vec_spec(d)]
    scratch = [pltpu.VMEM((2, tm * ROW_CHUNKS, LANES), F32), pltpu.SemaphoreType.DMA(())]
    args = (pos, pos, x, ys, wts, g.reshape(1, d))
    if final:
        return pl.pallas_call(
            _combine_final_kernel, grid=(nt,), in_specs=in_specs,
            out_specs=_row_spec(tm, d), out_shape=jax.ShapeDtypeStruct((m, d), F32),
            scratch_shapes=scratch, compiler_params=_cparams("arbitrary"), name="combine_final_norm",
        )(*args)
    return pl.pallas_call(
        _combine_rms_kernel, grid=(nt,), in_specs=in_specs,
        out_specs=[_row_spec(tm, d), _row_spec(tm, d)],
        out_shape=[jax.ShapeDtypeStruct((m, d), F32), jax.ShapeDtypeStruct((m, d), BF16)],
        scratch_shapes=scratch, compiler_params=_cparams("arbitrary"), name="combine_rmsnorm",
    )(*args)


def _gelu_exact(x):
    return 0.5 * x * (1.0 + lax.erf(x * (2.0 ** -0.5)))


def _mm_kernel(*refs, nk, act, has_bias, has_res):
    x_ref, w_ref = refs[0], refs[1]
    pos = 2
    b_ref = r_ref = None
    if has_bias:
        b_ref = refs[pos]
        pos += 1
    if has_res:
        r_ref = refs[pos]
        pos += 1
    o_ref = refs[pos]
    acc_ref = refs[pos + 1] if nk > 1 else None

    def finalize(acc):
        if has_bias:
            acc = acc + b_ref[...]
        if act == "gelu":
            acc = _gelu_exact(acc)
        if has_res:
            acc = acc + r_ref[...]
        o_ref[...] = acc.astype(o_ref.dtype)

    part = jnp.dot(x_ref[...], w_ref[...], preferred_element_type=F32)
    if nk == 1:
        finalize(part)
        return
    k = pl.program_id(2)

    @pl.when(k == 0)
    def _():
        acc_ref[...] = part

    @pl.when(jnp.logical_and(k > 0, k < nk - 1))
    def _():
        acc_ref[...] += part

    @pl.when(k == nk - 1)
    def _():
        finalize(acc_ref[...] + part)


def matmul(x, w, *, bias=None, res=None, act=None, out_dtype=F32, tm=1024, tn=512, tk=2048):
    m, kd = x.shape
    n = w.shape[1]
    tm, tn, tk = min(tm, m), min(tn, n), min(tk, kd)
    nk = kd // tk
    in_specs = [pl.BlockSpec((tm, tk), lambda i, j, k: (i, k)),
                pl.BlockSpec((tk, tn), lambda i, j, k: (k, j))]
    args = [x, w]
    if bias is not None:
        in_specs.append(pl.BlockSpec((1, tn), lambda i, j, k: (0, j)))
        args.append(bias.reshape(1, n).astype(F32))
    if res is not None:
        in_specs.append(pl.BlockSpec((tm, tn), lambda i, j, k: (i, j)))
        args.append(res)
    return pl.pallas_call(
        functools.partial(_mm_kernel, nk=nk, act=act, has_bias=bias is not None, has_res=res is not None),
        grid=(m // tm, n // tn, nk),
        in_specs=in_specs,
        out_specs=pl.BlockSpec((tm, tn), lambda i, j, k: (i, j)),
        out_shape=jax.ShapeDtypeStruct((m, n), out_dtype),
        scratch_shapes=[pltpu.VMEM((tm, tn), F32)] if nk > 1 else [],
        compiler_params=_cparams("parallel", "parallel", "arbitrary"), name="matmul",
    )(*args)


def _glu_kernel(x_ref, wv_ref, wg_ref, bv_ref, bg_ref, o_ref):
    x = x_ref[...]
    val = jnp.dot(x, wv_ref[...], preferred_element_type=F32) + bv_ref[...]
    gate = jnp.dot(x, wg_ref[...], preferred_element_type=F32) + bg_ref[...]
    o_ref[...] = val * jax.nn.sigmoid(gate)


def glu_matmul(x, w, b, *, tm=1024, tn=512):
    m, kd = x.shape
    n = w.shape[1] // 2
    nb = n // tn
    b2 = b.reshape(1, 2 * n).astype(F32)
    return pl.pallas_call(
        _glu_kernel, grid=(m // tm, nb),
        in_specs=[pl.BlockSpec((tm, kd), lambda i, j: (i, 0)),
                  pl.BlockSpec((kd, tn), lambda i, j: (0, j)),
                  pl.BlockSpec((kd, tn), lambda i, j: (0, j + nb)),
                  pl.BlockSpec((1, tn), lambda i, j: (0, j)),
                  pl.BlockSpec((1, tn), lambda i, j: (0, j + nb))],
        out_specs=pl.BlockSpec((tm, tn), lambda i, j: (i, j)),
        out_shape=jax.ShapeDtypeStruct((m, n), F32),
        compiler_params=_cparams("parallel", "parallel"), name="glu_matmul",
    )(x, w, w, b2, b2)


def _swiglu_part(x, wg_ref, wu_ref, wd_ref):
    g = jnp.dot(x, wg_ref[0], preferred_element_type=F32)
    u = jnp.dot(x, wu_ref[0], preferred_element_type=F32)
    a = (_silu(g) * u).astype(BF16)
    return jnp.dot(a, wd_ref[0], preferred_element_type=F32)


def _ffn_kernel(te_ref, tv_ref, x_ref, wg_ref, wu_ref, wd_ref, o_ref):
    f = pl.program_id(1)
    part = _swiglu_part(x_ref[...], wg_ref, wu_ref, wd_ref)

    @pl.when(f == 0)
    def _():
        o_ref[...] = part

    @pl.when(f > 0)
    def _():
        o_ref[...] += part


GATHER_ROWS_PER_STEP = 128


def _ffn_gather_kernel(te_ref, tv_ref, cur_ref, nxt_ref, h_hbm, wg_ref, wu_ref, wd_ref, o_ref,
                       xbuf_ref, xb_ref, acc_ref, sem, *, tm, nf, nt):
    i = pl.program_id(0)
    f = pl.program_id(1)

    @pl.when(jnp.logical_and(i == 0, f == 0))
    def _():
        def start(r, carry):
            _token_row_dma(h_hbm, cur_ref[r], xbuf_ref, r, sem).start()
            return carry
        lax.fori_loop(0, tm, start, 0, unroll=8)

    @pl.when(f == 0)
    def _():
        def wait(r, carry):
            _token_row_dma(h_hbm, cur_ref[r], xbuf_ref, r, sem).wait()
            return carry
        lax.fori_loop(0, tm, wait, 0, unroll=8)

        @pl.when(tv_ref[i] == 0)
        def _():
            o_ref[...] = jnp.zeros_like(o_ref)

        @pl.when(tv_ref[i] != 0)
        def _():
            for c in range(ROW_CHUNKS):
                xb_ref[:, c * LANES:(c + 1) * LANES] = xbuf_ref[_chunk_rows(c, tm), :].astype(BF16)

    @pl.when(jnp.logical_and(i + 1 < nt, jnp.logical_and(f >= 1, f <= tm // GATHER_ROWS_PER_STEP)))
    def _():
        for j in range(GATHER_ROWS_PER_STEP):
            r = (f - 1) * GATHER_ROWS_PER_STEP + j
            _token_row_dma(h_hbm, nxt_ref[r], xbuf_ref, r, sem).start()

    @pl.when(tv_ref[i] != 0)
    def _():
        part = _swiglu_part(xb_ref[...], wg_ref, wu_ref, wd_ref)

        @pl.when(f == 0)
        def _():
            acc_ref[...] = part

        @pl.when(jnp.logical_and(f > 0, f < nf - 1))
        def _():
            acc_ref[...] += part

        @pl.when(f == nf - 1)
        def _():
            for c in range(ROW_CHUNKS):
                cols = slice(c * LANES, (c + 1) * LANES)
                o_ref[_chunk_rows(c, tm), :] = acc_ref[:, cols] + part[:, cols]


def _ffn_weight_specs(d, tf, nf):
    def wcol(i, f, te, tv):
        return (te[i], 0, jnp.where(tv[i] != 0, f, nf - 1))

    def wrow(i, f, te, tv):
        return (te[i], jnp.where(tv[i] != 0, f, nf - 1), 0)

    return [pl.BlockSpec((1, d, tf), wcol), pl.BlockSpec((1, d, tf), wcol), pl.BlockSpec((1, tf, d), wrow)]


def grouped_ffn(x, w_gate, w_up, w_down, tile_expert, tile_valid, *, tm, tf=512):
    mp, d = x.shape
    nf = w_gate.shape[2] // tf
    row_block = pl.BlockSpec((tm, d), lambda i, f, te, tv: (i, 0))
    grid_spec = pltpu.PrefetchScalarGridSpec(
        num_scalar_prefetch=2, grid=(mp // tm, nf),
        in_specs=[row_block] + _ffn_weight_specs(d, tf, nf), out_specs=row_block)
    return pl.pallas_call(
        _ffn_kernel, grid_spec=grid_spec, out_shape=jax.ShapeDtypeStruct((mp, d), F32),
        compiler_params=_cparams("parallel", "arbitrary"), name="grouped_ffn",
    )(tile_expert, tile_valid, x, w_gate, w_up, w_down)


def gathered_ffn(h, row_token, w_gate, w_up, w_down, tile_expert, tile_valid, *, tm, tf=512):
    d = w_gate.shape[1]
    mp = row_token.shape[0]
    nt = mp // tm
    nf = w_gate.shape[2] // tf
    assert tm % GATHER_ROWS_PER_STEP == 0 and tm // GATHER_ROWS_PER_STEP <= nf - 1
    smem_rows = functools.partial(pl.BlockSpec, (tm,), memory_space=pltpu.SMEM)
    grid_spec = pltpu.PrefetchScalarGridSpec(
        num_scalar_prefetch=2, grid=(nt, nf),
        in_specs=[smem_rows(lambda i, f, te, tv: (i,)),
                  smem_rows(lambda i, f, te, tv: (jnp.minimum(i + 1, nt - 1),)),
                  pl.BlockSpec(memory_space=pl.ANY)] + _ffn_weight_specs(d, tf, nf),
        out_specs=pl.BlockSpec((tm * ROW_CHUNKS, LANES), lambda i, f, te, tv: (i, 0)),
        scratch_shapes=[pltpu.VMEM((tm * ROW_CHUNKS, LANES), F32), pltpu.VMEM((tm, d), BF16),
                        pltpu.VMEM((tm, d), F32), pltpu.SemaphoreType.DMA(())])
    return pl.pallas_call(
        functools.partial(_ffn_gather_kernel, tm=tm, nf=nf, nt=nt), grid_spec=grid_spec,
        out_shape=jax.ShapeDtypeStruct((mp * ROW_CHUNKS, LANES), F32),
        compiler_params=_cparams("arbitrary", "arbitrary"), name="gathered_ffn",
    )(tile_expert, tile_valid, row_token, row_token, h, w_gate, w_up, w_down)


def _split_bf16(x):
    hi = x.astype(BF16)
    lo = (x - hi.astype(F32)).astype(BF16)
    return hi, lo


def _router_kernel(x_ref, g_ref, rhi_ref, rlo_ref, h_ref, mi_ref, mf_ref, cnt_ref, carry_ref, *, tm):
    i = pl.program_id(0)

    @pl.when(i == 0)
    def _():
        carry_ref[...] = jnp.zeros_like(carry_ref)

    h = _rms(x_ref[...], g_ref[...])
    for c in range(ROW_CHUNKS):
        h_ref[_chunk_rows(c, tm), :] = h[:, c * LANES:(c + 1) * LANES]
    h_hi, h_lo = _split_bf16(h)
    logits = (jnp.dot(h_hi, rhi_ref[...], preferred_element_type=F32)
              + (jnp.dot(h_lo, rhi_ref[...], preferred_element_type=F32)
                 + jnp.dot(h_hi, rlo_ref[...], preferred_element_type=F32)))
    lane = lax.broadcasted_iota(jnp.int32, (tm, LANES), 1)
    lg = jnp.where(lane < N_EXPERTS, logits, NEG_INF)
    m1 = jnp.max(lg, axis=1, keepdims=True)
    i1 = jnp.min(jnp.where(lg == m1, lane, LANES), axis=1, keepdims=True)
    lg2 = jnp.where(lane == i1, NEG_INF, lg)
    m2 = jnp.max(lg2, axis=1, keepdims=True)
    i2 = jnp.min(jnp.where(lg2 == m2, lane, LANES), axis=1, keepdims=True)
    e2 = jnp.exp(m2 - m1)
    w1 = 1.0 / (1.0 + e2)
    w2 = e2 / (1.0 + e2)
    oh1 = (lane == i1).astype(F32)
    oh2 = (lane == i2).astype(F32)
    tot = oh1 + oh2
    r = lax.broadcasted_iota(jnp.int32, (tm, tm), 0)
    c = lax.broadcasted_iota(jnp.int32, (tm, tm), 1)
    before = jnp.dot((c < r).astype(BF16), tot.astype(BF16), preferred_element_type=F32)
    base = before + carry_ref[0:1, :]
    rank1 = jnp.sum(oh1 * base, axis=1, keepdims=True).astype(jnp.int32)
    rank2 = jnp.sum(oh2 * base, axis=1, keepdims=True).astype(jnp.int32)
    mi_ref[...] = jnp.where(lane == 0, i1, jnp.where(lane == 1, i2, jnp.where(lane == 2, rank1, rank2)))
    mf_ref[...] = jnp.where(lane == 0, w1, w2)
    carry = carry_ref[...] + jnp.sum(tot, axis=0, keepdims=True)
    carry_ref[...] = carry
    cnt_ref[...] = carry


def route(x, g, router, *, tm=512):
    m, d = x.shape
    rpad = jnp.zeros((d, LANES), F32).at[:, :N_EXPERTS].set(router.astype(F32))
    rhi, rlo = _split_bf16(rpad)
    return pl.pallas_call(
        functools.partial(_router_kernel, tm=tm), grid=(m // tm,),
        in_specs=[_row_spec(tm, d), _vec_spec(d),
                  pl.BlockSpec((d, LANES), lambda i: (0, 0)), pl.BlockSpec((d, LANES), lambda i: (0, 0))],
        out_specs=[_row_spec(tm * ROW_CHUNKS, LANES), _row_spec(tm, LANES), _row_spec(tm, LANES),
                   pl.BlockSpec((8, LANES), lambda i: (0, 0))],
        out_shape=[jax.ShapeDtypeStruct((m * ROW_CHUNKS, LANES), F32), jax.ShapeDtypeStruct((m, LANES), jnp.int32),
                   jax.ShapeDtypeStruct((m, LANES), F32), jax.ShapeDtypeStruct((8, LANES), F32)],
        scratch_shapes=[pltpu.VMEM((8, LANES), F32)],
        compiler_params=_cparams("arbitrary"), name="moe_route",
    )(x, g.reshape(1, d), rhi, rlo)


def moe_layer(x, g, router, w_gate, w_up, w_down, *, expert_base=0, tm=1024):
    n, d = x.shape
    h, meta_i, wts, counts = route(x, g, router)
    e1, e2, r1, r2 = (meta_i[:, k] for k in range(4))
    cnt = counts[0, :N_EXPERTS].astype(jnp.int32)
    padded = ((cnt + tm - 1) // tm) * tm
    ends = jnp.cumsum(padded)
    starts = ends - padded
    pos1 = starts[e1] + r1
    pos2 = starts[e2] + r2
    n_tiles = (2 * n) // tm + N_EXPERTS
    tile_start = jnp.arange(n_tiles, dtype=jnp.int32) * tm
    tile_expert = jnp.minimum(jnp.sum((tile_start[:, None] >= ends[None, :]).astype(jnp.int32), axis=1),
                              N_EXPERTS - 1)
    tile_valid = (tile_start < ends[-1]).astype(jnp.int32)
    tok = jnp.arange(n, dtype=jnp.int32)
    pos = jnp.concatenate([pos1, pos2])
    row_token = jnp.zeros((n_tiles * tm,), jnp.int32).at[pos].set(jnp.concatenate([tok, tok]))
    ys = gathered_ffn(h, row_token, w_gate, w_up, w_down, tile_expert + expert_base, tile_valid, tm=tm)
    return ys, pos, wts


def _gdn_gates_kernel(ba_ref, alog_ref, dt_ref, o_ref, *, tm):
    ba = ba_ref[...]
    sp = ba + dt_ref[...]
    softplus = jnp.maximum(sp, 0.0) + jnp.log1p(jnp.exp(-jnp.abs(sp)))
    gcum = -jnp.exp(alog_ref[...]) * softplus
    inchunk = lax.broadcasted_iota(jnp.int32, (tm, LANES), 0) % GDN_CHUNK
    sh = 1
    while sh < GDN_CHUNK:
        gcum = gcum + jnp.where(inchunk >= sh, pltpu.roll(gcum, sh, 0), 0.0)
        sh *= 2
    lane = lax.broadcasted_iota(jnp.int32, (tm, LANES), 1)
    o_ref[...] = jnp.where(lane < GDN_V_HEADS, jax.nn.sigmoid(ba), gcum)


def gdn_gates(ba, a_log, dt_bias, *, tm=512):
    m = ba.shape[0]
    alog_row = jnp.zeros((1, LANES), F32).at[0, GDN_V_HEADS:2 * GDN_V_HEADS].set(a_log.astype(F32))
    dt_row = jnp.zeros((1, LANES), F32).at[0, GDN_V_HEADS:2 * GDN_V_HEADS].set(dt_bias.astype(F32))
    return pl.pallas_call(
        functools.partial(_gdn_gates_kernel, tm=tm), grid=(m // tm,),
        in_specs=[_row_spec(tm, LANES), _vec_spec(LANES), _vec_spec(LANES)],
        out_specs=_row_spec(tm, LANES), out_shape=jax.ShapeDtypeStruct((m, LANES), F32),
        compiler_params=_cparams("parallel"), name="gdn_gates",
    )(ba, alog_row, dt_row)


def _gdn_kernel(q_ref, k_ref, v_ref, z_ref, gates_ref, cwq_ref, cwk_ref, cwv_ref, nw_ref,
                o_ref, state_ref, hq_ref, hk_ref, hv_ref, xq_ref, xk_ref, xv_ref, *, ts, n_pairs):
    first_head = pl.program_id(1) * (2 * n_pairs)
    s = pl.program_id(2)
    hd = HEAD_DIM

    @pl.when(s == 0)
    def _():
        state_ref[...] = jnp.zeros_like(state_ref)
        hq_ref[...] = jnp.zeros_like(hq_ref)
        hk_ref[...] = jnp.zeros_like(hk_ref)
        hv_ref[...] = jnp.zeros_like(hv_ref)

    def conv_silu(x_ref, hist_ref, xx_ref, w_ref):
        xx_ref[0:8, :] = hist_ref[...]
        xx_ref[8:, :] = x_ref[...]
        hist_ref[...] = x_ref[ts - 8:, :]
        w = w_ref[...]
        acc = w[0:1, :] * xx_ref[5:5 + ts, :]
        for j in range(1, GDN_CONV):
            acc = acc + w[j:j + 1, :] * xx_ref[5 + j:5 + j + ts, :]
        return _silu(acc)

    def l2n(x):
        return x * lax.rsqrt(jnp.sum(x * x, axis=-1, keepdims=True) + EPS)

    q_all = conv_silu(q_ref, hq_ref, xq_ref, cwq_ref)
    k_all = conv_silu(k_ref, hk_ref, xk_ref, cwk_ref)
    v = conv_silu(v_ref, hv_ref, xv_ref, cwv_ref)
    q = [l2n(q_all[:, g * hd:(g + 1) * hd]) * (hd ** -0.5) for g in range(n_pairs)]
    k = [l2n(k_all[:, g * hd:(g + 1) * hd]) for g in range(n_pairs)]
    heads = range(2 * n_pairs)

    gates = gates_ref[...]
    lane = lax.broadcasted_iota(jnp.int32, (ts, LANES), 1)

    def pick(idx):
        return jnp.sum(jnp.where(lane == idx, gates, 0.0), axis=1, keepdims=True)

    beta = [pick(first_head + h) for h in heads]
    gc = [pick(GDN_V_HEADS + first_head + h) for h in heads]
    egc = [jnp.exp(x) for x in gc]

    sc = 2 * GDN_CHUNK
    r128 = lax.broadcasted_iota(jnp.int32, (sc, sc), 0)
    c128 = lax.broadcasted_iota(jnp.int32, (sc, sc), 1)
    same_chunk = (r128 // GDN_CHUNK) == (c128 // GDN_CHUNK)
    incl = jnp.logical_and(same_chunk, c128 <= r128)
    strict = jnp.logical_and(same_chunk, c128 < r128)

    def mm(a, b):
        return jnp.dot(a.astype(BF16), b.astype(BF16), preferred_element_type=F32)

    qb = [x.astype(BF16) for x in q]
    kb = [x.astype(BF16) for x in k]
    n_blk = ts // sc
    chains = [(u, h) for u in range(n_blk) for h in heads]
    blk_rows = [slice(u * sc, (u + 1) * sc) for u in range(n_blk)]
    ab = [[lax.dot_general(jnp.concatenate([qb[g][r], kb[g][r]], axis=0), kb[g][r], (((1,), (1,)), ((), ())),
                           preferred_element_type=F32) for r in blk_rows] for g in range(n_pairs)]
    kt = [[k[g][r].T.astype(BF16) for r in blk_rows] for g in range(n_pairs)]
    eye = (r128 == c128).astype(F32)
    same16 = (r128 // 16) == (c128 // 16)
    same32 = (r128 // 32) == (c128 // 32)
    lmat, attn, ymat = [], [], []
    for u, h in chains:
        r = blk_rows[u]
        g = h // 2
        gcol, bcol, ecol = gc[h][r], beta[h][r], egc[h][r]
        grow = jnp.broadcast_to(gcol, (sc, sc)).T
        decay = jnp.exp(jnp.where(incl, gcol - grow, NEG_INF))
        lmat.append(jnp.where(strict, ab[g][u][sc:] * bcol * decay, 0.0))
        attn.append(ab[g][u][:sc] * decay)
        ymat.append(jnp.concatenate([bcol * v[r, h * hd:(h + 1) * hd], (bcol * ecol) * k[g][r]], axis=1))
    nd = [jnp.where(same16, -m, 0.0) for m in lmat]
    x = [eye + m for m in nd]
    pw = [mm(m, m) for m in nd]
    for level in range(3):
        xp = [mm(a, b) for a, b in zip(x, pw)]
        if level < 2:
            pw = [mm(m, m) for m in pw]
        x = [a + b for a, b in zip(x, xp)]
    for inner, outer in ((same16, same32), (same32, same_chunk)):
        off = [jnp.where(jnp.logical_and(outer, jnp.logical_not(inner)), m, 0.0) for m in lmat]
        t1 = [mm(a, b) for a, b in zip(x, off)]
        t2 = [mm(a, b) for a, b in zip(t1, x)]
        x = [a - b for a, b in zip(x, t2)]
    uw = [mm(a, b) for a, b in zip(x, ymat)]
    dw = [mm(a, b) for a, b in zip(attn, uw)]
    steps = [[] for _ in heads]
    for i, (u, h) in enumerate(chains):
        r = blk_rows[u]
        cmat = egc[h][r] * q[h // 2][r] - dw[i][:, hd:]
        for c in range(2):
            lrow = slice(c * GDN_CHUNK, (c + 1) * GDN_CHUNK)
            last = u * sc + (c + 1) * GDN_CHUNK - 1
            gl = gc[h][last:last + 1]
            tail = jnp.exp(gl - gc[h][r][lrow]) * uw[i][lrow]
            zeros = jnp.zeros_like(tail)
            full = jnp.concatenate([tail, zeros] if c == 0 else [zeros, tail], axis=0)
            bm = jnp.dot(kt[h // 2][u], full.astype(BF16), preferred_element_type=F32)
            mc = jnp.concatenate([bm[:, hd:], cmat[lrow]], axis=0).astype(BF16)
            steps[h].append((mc, bm[:, :hd], dw[i][lrow, :hd], jnp.exp(gl)))

    states = [state_ref[h] for h in heads]
    outs = [[] for _ in heads]
    for cc in range(ts // GDN_CHUNK):
        for h in heads:
            mc, bmat, dmat, glast = steps[h][cc]
            ms = jnp.dot(mc, states[h].astype(BF16), preferred_element_type=F32)
            outs[h].append(ms[hd:] + dmat)
            states[h] = glast * states[h] + (bmat - ms[:hd])
    for h in heads:
        state_ref[h] = states[h]

    z = z_ref[...]
    nw = nw_ref[...]
    ys = []
    for h in heads:
        o = jnp.concatenate(outs[h], axis=0)
        ys.append(_rms(o, nw) * _silu(z[:, h * hd:(h + 1) * hd]))
    o_ref[...] = jnp.concatenate(ys, axis=1).astype(o_ref.dtype)


def gdn_core(proj, gates, conv_w, norm_w, *, batch, seq, ts=256, n_pairs=4):
    hd = HEAD_DIM
    ns = seq // ts
    qw, vw = n_pairs * hd, 2 * n_pairs * hd
    kq = GDN_QK_DIM // qw
    vb = (2 * GDN_QK_DIM) // vw
    zb = GDN_CONV_DIM // vw

    def rows(b, p, s):
        return b * ns + s

    in_specs = [
        pl.BlockSpec((ts, qw), lambda b, p, s: (rows(b, p, s), p)),
        pl.BlockSpec((ts, qw), lambda b, p, s: (rows(b, p, s), kq + p)),
        pl.BlockSpec((ts, vw), lambda b, p, s: (rows(b, p, s), vb + p)),
        pl.BlockSpec((ts, vw), lambda b, p, s: (rows(b, p, s), zb + p)),
        pl.BlockSpec((ts, LANES), lambda b, p, s: (rows(b, p, s), 0)),
        pl.BlockSpec((GDN_CONV, qw), lambda b, p, s: (0, p)),
        pl.BlockSpec((GDN_CONV, qw), lambda b, p, s: (0, kq + p)),
        pl.BlockSpec((GDN_CONV, vw), lambda b, p, s: (0, vb + p)),
        pl.BlockSpec((1, hd), lambda b, p, s: (0, 0)),
    ]
    return pl.pallas_call(
        functools.partial(_gdn_kernel, ts=ts, n_pairs=n_pairs), grid=(batch, GDN_QK_HEADS // n_pairs, ns),
        in_specs=in_specs,
        out_specs=pl.BlockSpec((ts, vw), lambda b, p, s: (rows(b, p, s), p)),
        out_shape=jax.ShapeDtypeStruct((batch * seq, GDN_V_DIM), BF16),
        scratch_shapes=[pltpu.VMEM((2 * n_pairs, hd, hd), F32),
                        pltpu.VMEM((8, qw), F32), pltpu.VMEM((8, qw), F32), pltpu.VMEM((8, vw), F32),
                        pltpu.VMEM((ts + 8, qw), F32), pltpu.VMEM((ts + 8, qw), F32),
                        pltpu.VMEM((ts + 8, vw), F32)],
        compiler_params=_cparams("parallel", "parallel", "arbitrary"), name="gdn_core",
    )(proj, proj, proj, proj, gates, conv_w, conv_w, conv_w, norm_w.reshape(1, hd).astype(F32))


def gated_deltanet(x, h, w_in, conv_w, a_log, dt_bias, norm_w, w_out, *, batch, seq):
    main = GDN_CONV_DIM + GDN_V_DIM
    w_main = w_in[:, :main].astype(BF16)
    w_ba = jnp.zeros((w_in.shape[0], LANES), F32).at[:, :2 * GDN_V_HEADS].set(w_in[:, main:]).astype(BF16)
    proj = matmul(h, w_main)
    gates = gdn_gates(matmul(h, w_ba), a_log, dt_bias)
    y = gdn_core(proj, gates, conv_w.astype(F32), norm_w, batch=batch, seq=seq)
    return matmul(y, w_out.astype(BF16), res=x)


def _sgu_kernel(u_ref, v_ref, g_ref, b_ref, ws_ref, bs_ref, o_ref, *, nch):
    r = lax.broadcasted_iota(jnp.int32, (SGU_CHUNK, SGU_CHUNK), 0)
    c = lax.broadcasted_iota(jnp.int32, (SGU_CHUNK, SGU_CHUNK), 1)
    causal = c <= r
    gd = SGU_GROUP_DIM
    for ch in range(nch):
        rows = slice(ch * SGU_CHUNK, (ch + 1) * SGU_CHUNK)
        v = v_ref[rows, :]
        mu = jnp.mean(v, axis=-1, keepdims=True)
        vc = v - mu
        vn = vc * lax.rsqrt(jnp.mean(vc * vc, axis=-1, keepdims=True) + LN_EPS)
        vn = (vn * g_ref[...] + b_ref[...]).astype(BF16)
        for grp in range(SGU_GROUPS):
            cols = slice(grp * gd, (grp + 1) * gd)
            w = jnp.where(causal, ws_ref[grp], 0.0).astype(BF16)
            sv = jnp.dot(w, vn[:, cols], preferred_element_type=F32) + bs_ref[:, grp:grp + 1]
            o_ref[rows, cols] = (u_ref[rows, cols] * sv).astype(o_ref.dtype)


def sgu_spatial(p, ln_g, ln_b, w_s, b_s, *, nch=2):
    m = p.shape[0]
    rows = nch * SGU_CHUNK
    inner = SGU_INNER
    return pl.pallas_call(
        functools.partial(_sgu_kernel, nch=nch), grid=(m // rows,),
        in_specs=[pl.BlockSpec((rows, inner), lambda i: (i, 0)),
                  pl.BlockSpec((rows, inner), lambda i: (i, 1)),
                  _vec_spec(inner), _vec_spec(inner),
                  pl.BlockSpec((SGU_GROUPS, SGU_CHUNK, SGU_CHUNK), lambda i: (0, 0, 0)),
                  pl.BlockSpec((SGU_CHUNK, SGU_GROUPS), lambda i: (0, 0))],
        out_specs=pl.BlockSpec((rows, inner), lambda i: (i, 0)),
        out_shape=jax.ShapeDtypeStruct((m, inner), BF16),
        compiler_params=_cparams("parallel"), name="sgu_spatial",
    )(p, p, ln_g.reshape(1, inner).astype(F32), ln_b.reshape(1, inner).astype(F32),
      w_s.astype(F32), b_s.T.astype(F32))


def chunked_sgu(x, h, w_in, b_in, ln_g, ln_b, w_s, b_s, w_out, b_out):
    p = matmul(h, w_in.astype(BF16), bias=b_in, act="gelu")
    uv = sgu_spatial(p, ln_g, ln_b, w_s, b_s)
    return matmul(uv, w_out.astype(BF16), bias=b_out, res=x)


CV_HALO = 32


def _cvconv_kernel(halo_ref, y_ref, w_ref, bdw_ref, g_ref, b_ref, o_ref, xx_ref, yc_ref, *, ts, cb):
    s = pl.program_id(1)
    d = y_ref.shape[1]

    @pl.when(s == 0)
    def _():
        xx_ref[0:CV_HALO, :] = jnp.zeros((CV_HALO, d), F32)

    @pl.when(s > 0)
    def _():
        xx_ref[0:CV_HALO, :] = halo_ref[...]

    xx_ref[CV_HALO:, :] = y_ref[...]
    off = CV_HALO - (CONV_WIDTH - 1)
    for blk in range(d // cb):
        cols = slice(blk * cb, (blk + 1) * cb)
        w = w_ref[:, cols]
        acc = w[0:1, :] * xx_ref[off:off + ts, cols]
        for j in range(1, CONV_WIDTH):
            acc = acc + w[j:j + 1, :] * xx_ref[off + j:off + j + ts, cols]
        yc_ref[:, cols] = acc + bdw_ref[:, cols]
    y = yc_ref[...]
    mu = jnp.mean(y, axis=-1, keepdims=True)
    yc = y - mu
    yn = yc * lax.rsqrt(jnp.mean(yc * yc, axis=-1, keepdims=True) + LN_EPS)
    o_ref[...] = _silu(yn * g_ref[...] + b_ref[...]).astype(o_ref.dtype)


def conformer_dwconv(y, w_dw, b_dw, ln_g, ln_b, *, batch, seq, ts=128, cb=256):
    m, d = y.shape
    ns = seq // ts
    hb = ts // CV_HALO
    return pl.pallas_call(
        functools.partial(_cvconv_kernel, ts=ts, cb=cb), grid=(batch, ns),
        in_specs=[pl.BlockSpec((CV_HALO, d), lambda b, s: (jnp.maximum((b * ns + s) * hb - 1, 0), 0)),
                  pl.BlockSpec((ts, d), lambda b, s: (b * ns + s, 0)),
                  pl.BlockSpec((CONV_WIDTH, d), lambda b, s: (0, 0)),
                  pl.BlockSpec((1, d), lambda b, s: (0, 0)),
                  pl.BlockSpec((1, d), lambda b, s: (0, 0)),
                  pl.BlockSpec((1, d), lambda b, s: (0, 0))],
        out_specs=pl.BlockSpec((ts, d), lambda b, s: (b * ns + s, 0)),
        out_shape=jax.ShapeDtypeStruct((m, d), BF16),
        scratch_shapes=[pltpu.VMEM((ts + CV_HALO, d), F32), pltpu.VMEM((ts, d), F32)],
        compiler_params=_cparams("parallel", "parallel"), name="conformer_dwconv",
    )(y, y, w_dw.astype(F32), b_dw.reshape(1, d).astype(F32), ln_g.reshape(1, d).astype(F32),
      ln_b.reshape(1, d).astype(F32))


def conformer_conv(x, h, w_pw1, b_pw1, w_dw, b_dw, ln_g, ln_b, w_pw2, b_pw2, *, batch, seq):
    y = glu_matmul(h, w_pw1.astype(BF16), b_pw1)
    yn = conformer_dwconv(y, w_dw, b_dw, ln_g, ln_b, batch=batch, seq=seq)
    return matmul(yn, w_pw2.astype(BF16), bias=b_pw2, res=x)


def dense_ffn(h, w_gate, w_up, w_down, *, tm=1024):
    m = h.shape[0]
    nt = m // tm
    return grouped_ffn(h, w_gate[None].astype(BF16), w_up[None].astype(BF16), w_down[None].astype(BF16),
                       jnp.zeros((nt,), jnp.int32), jnp.ones((nt,), jnp.int32), tm=tm)


def kernel(x, mix_norm, ffn_norm, gdn_w_in, gdn_conv_w, gdn_a_log, gdn_dt_bias, gdn_norm, gdn_w_out, sgu_w_in, sgu_b_in, sgu_ln_g, sgu_ln_b, sgu_w_s, sgu_b_s, sgu_w_out, sgu_b_out, cv_w_pw1, cv_b_pw1, cv_w_dw, cv_b_dw, cv_ln_g, cv_ln_b, cv_w_pw2, cv_b_pw2, ffn_w_gate, ffn_w_up, ffn_w_down, moe_router, moe_w_gate, moe_w_up, moe_w_down, final_norm):
    batch, seq, d = x.shape
    depth = mix_norm.shape[0]
    xr = x.reshape(batch * seq, d).astype(F32)
    pending = None
    moe_wg, moe_wu, moe_wd = (w.astype(BF16).reshape((-1,) + w.shape[2:]) for w in (moe_w_gate, moe_w_up, moe_w_down))
    for i in range(depth):
        if pending is None:
            h = rmsnorm(xr, mix_norm[i])
        elif pending[0] == "dense":
            xr, h = add_rmsnorm(xr, pending[1], mix_norm[i])
        else:
            xr, h = combine_rmsnorm(xr, *pending[1:], mix_norm[i], final=False)
        kind, j = i % N_MIXERS, i // N_MIXERS
        if kind == 0:
            xr = gated_deltanet(xr, h, gdn_w_in[j], gdn_conv_w[j], gdn_a_log[j], gdn_dt_bias[j], gdn_norm[j],
                                gdn_w_out[j], batch=batch, seq=seq)
        elif kind == 1:
            xr = chunked_sgu(xr, h, sgu_w_in[j], sgu_b_in[j], sgu_ln_g[j], sgu_ln_b[j], sgu_w_s[j], sgu_b_s[j],
                             sgu_w_out[j], sgu_b_out[j])
        else:
            xr = conformer_conv(xr, h, cv_w_pw1[j], cv_b_pw1[j], cv_w_dw[j], cv_b_dw[j], cv_ln_g[j], cv_ln_b[j],
                                cv_w_pw2[j], cv_b_pw2[j], batch=batch, seq=seq)
        j = i // 2
        if i % 2 == 0:
            h = rmsnorm(xr, ffn_norm[i])
            pending = ("dense", dense_ffn(h, ffn_w_gate[j], ffn_w_up[j], ffn_w_down[j]))
        else:
            pending = ("moe",) + moe_layer(xr, ffn_norm[i], moe_router[j], moe_wg, moe_wu, moe_wd,
                                           expert_base=j * N_EXPERTS)
    if pending[0] == "dense":
        xr, _ = add_rmsnorm(xr, pending[1], final_norm)
        out = _final_rms(xr, final_norm)
    else:
        out = combine_rmsnorm(xr, *pending[1:], final_norm, final=True)
    return out.reshape(batch, seq, d).astype(x.dtype)


def _final_rms(x, g, *, tm=512):
    m, d = x.shape
    return pl.pallas_call(
        _rms_kernel, grid=(m // tm,),
        in_specs=[_row_spec(tm, d), _vec_spec(d)],
        out_specs=_row_spec(tm, d),
        out_shape=jax.ShapeDtypeStruct((m, d), F32),
        compiler_params=_cparams("parallel"), name="final_rmsnorm",
    )(x, g.reshape(1, d))
```

```python
import functools

import jax
import jax.numpy as jnp
from jax import lax
from jax.experimental import pallas as pl
from jax.experimental.pallas import tpu as pltpu

BF16 = jnp.bfloat16
F32 = jnp.float32

D_MODEL = 2048
HEAD_DIM = 128
GDN_QK_HEADS = D_MODEL // HEAD_DIM
GDN_V_HEADS = 2 * GDN_QK_HEADS
GDN_QK_DIM = GDN_QK_HEADS * HEAD_DIM
GDN_V_DIM = GDN_V_HEADS * HEAD_DIM
GDN_CONV_DIM = 2 * GDN_QK_DIM + GDN_V_DIM
GDN_CONV = 4
GDN_CHUNK = 64
SGU_CHUNK = 128
SGU_INNER = 2 * D_MODEL
SGU_GROUPS = 16
SGU_GROUP_DIM = SGU_INNER // SGU_GROUPS
CONV_WIDTH = 31
N_EXPERTS = 8
N_MIXERS = 3
EPS = 1e-6
LN_EPS = 1e-5

LANES = 128
ROW_CHUNKS = D_MODEL // LANES
VMEM_LIMIT = 56 * 1024 * 1024
NEG_INF = float("-inf")


def _cparams(*sem):
    return pltpu.CompilerParams(dimension_semantics=sem, vmem_limit_bytes=VMEM_LIMIT)


def _silu(x):
    return x * jax.nn.sigmoid(x)


def _rms(x, g):
    return x * lax.rsqrt(jnp.mean(x * x, axis=-1, keepdims=True) + EPS) * g


def _rms_kernel(x_ref, g_ref, h_ref):
    h_ref[...] = _rms(x_ref[...], g_ref[...]).astype(h_ref.dtype)


def _add_rms_kernel(x_ref, d_ref, g_ref, xo_ref, h_ref):
    x = x_ref[...] + d_ref[...]
    xo_ref[...] = x
    h_ref[...] = _rms(x, g_ref[...]).astype(h_ref.dtype)


def _chunk_rows(c, tm):
    return pl.ds(c, tm, stride=ROW_CHUNKS)


def _token_row_dma(src_hbm, tok, dst_ref, r, sem):
    s = pl.multiple_of(tok * ROW_CHUNKS, ROW_CHUNKS)
    return pltpu.make_async_copy(src_hbm.at[pl.ds(s, ROW_CHUNKS)], dst_ref.at[pl.ds(r * ROW_CHUNKS, ROW_CHUNKS)], sem)


def _combine(x_ref, pos1_ref, pos2_ref, ys_hbm, w_ref, obuf_ref, sem):
    tm = x_ref.shape[0]

    def copies(r):
        return (_token_row_dma(ys_hbm, pos1_ref[r], obuf_ref.at[0], r, sem),
                _token_row_dma(ys_hbm, pos2_ref[r], obuf_ref.at[1], r, sem))

    def start(r, carry):
        for cp in copies(r):
            cp.start()
        return carry

    def wait(r, carry):
        for cp in copies(r):
            cp.wait()
        return carry

    lax.fori_loop(0, tm, start, 0, unroll=8)
    lax.fori_loop(0, tm, wait, 0, unroll=8)
    w = w_ref[...]
    w1, w2 = w[:, 0:1], w[:, 1:2]
    parts = []
    for c in range(ROW_CHUNKS):
        rows = _chunk_rows(c, tm)
        parts.append(x_ref[:, c * LANES:(c + 1) * LANES] + (w1 * obuf_ref[0, rows, :] + w2 * obuf_ref[1, rows, :]))
    return jnp.concatenate(parts, axis=1)


def _combine_rms_kernel(pos1_ref, pos2_ref, x_ref, ys_hbm, w_ref, g_ref, xo_ref, h_ref, obuf_ref, sem):
    x = _combine(x_ref, pos1_ref, pos2_ref, ys_hbm, w_ref, obuf_ref, sem)
    xo_ref[...] = x
    h_ref[...] = _rms(x, g_ref[...]).astype(h_ref.dtype)


def _combine_final_kernel(pos1_ref, pos2_ref, x_ref, ys_hbm, w_ref, g_ref, h_ref, obuf_ref, sem):
    x = _combine(x_ref, pos1_ref, pos2_ref, ys_hbm, w_ref, obuf_ref, sem)
    h_ref[...] = _rms(x, g_ref[...]).astype(h_ref.dtype)


def _row_spec(tm, d):
    return pl.BlockSpec((tm, d), lambda i: (i, 0))


def _vec_spec(d):
    return pl.BlockSpec((1, d), lambda i: (0, 0))


def rmsnorm(x, g, *, tm=512):
    m, d = x.shape
    return pl.pallas_call(
        _rms_kernel, grid=(m // tm,),
        in_specs=[_row_spec(tm, d), _vec_spec(d)],
        out_specs=_row_spec(tm, d),
        out_shape=jax.ShapeDtypeStruct((m, d), BF16),
        compiler_params=_cparams("parallel"), name="rmsnorm",
    )(x, g.reshape(1, d))


def add_rmsnorm(x, delta, g, *, tm=512):
    m, d = x.shape
    return pl.pallas_call(
        _add_rms_kernel, grid=(m // tm,),
        in_specs=[_row_spec(tm, d), _row_spec(tm, d), _vec_spec(d)],
        out_specs=[_row_spec(tm, d), _row_spec(tm, d)],
        out_shape=[jax.ShapeDtypeStruct((m, d), F32), jax.ShapeDtypeStruct((m, d), BF16)],
        compiler_params=_cparams("parallel"), name="add_rmsnorm",
    )(x, delta, g.reshape(1, d))


def combine_rmsnorm(x, ys, pos, wts, g, *, final, tm=256):
    m, d = x.shape
    nt = m // tm
    in_specs = [pl.BlockSpec((tm,), lambda i: (i,), memory_space=pltpu.SMEM),
                pl.BlockSpec((tm,), lambda i: (i + nt,), memory_space=pltpu.SMEM),
                _row_spec(tm, d), pl.BlockSpec(memory_space=pl.ANY), _row_spec(tm, LANES), _vec_spec(d)]
    scratch = [pltpu.VMEM((2, tm * ROW_CHUNKS, LANES), F32), pltpu.SemaphoreType.DMA(())]
    args = (pos, pos, x, ys, wts, g.reshape(1, d))
    if final:
        return pl.pallas_call(
            _combine_final_kernel, grid=(nt,), in_specs=in_specs,
            out_specs=_row_spec(tm, d), out_shape=jax.ShapeDtypeStruct((m, d), F32),
            scratch_shapes=scratch, compiler_params=_cparams("arbitrary"), name="combine_final_norm",
        )(*args)
    return pl.pallas_call(
        _combine_rms_kernel, grid=(nt,), in_specs=in_specs,
        out_specs=[_row_spec(tm, d), _row_spec(tm, d)],
        out_shape=[jax.ShapeDtypeStruct((m, d), F32), jax.ShapeDtypeStruct((m, d), BF16)],
        scratch_shapes=scratch, compiler_params=_cparams("arbitrary"), name="combine_rmsnorm",
    )(*args)


def _gelu_exact(x):
    return 0.5 * x * (1.0 + lax.erf(x * (2.0 ** -0.5)))


def _mm_kernel(*refs, nk, act, has_bias, has_res):
    x_ref, w_ref = refs[0], refs[1]
    pos = 2
    b_ref = r_ref = None
    if has_bias:
        b_ref = refs[pos]
        pos += 1
    if has_res:
        r_ref = refs[pos]
        pos += 1
    o_ref = refs[pos]
    acc_ref = refs[pos + 1] if nk > 1 else None

    def finalize(acc):
        if has_bias:
            acc = acc + b_ref[...]
        if act == "gelu":
            acc = _gelu_exact(acc)
        if has_res:
            acc = acc + r_ref[...]
        o_ref[...] = acc.astype(o_ref.dtype)

    part = jnp.dot(x_ref[...], w_ref[...], preferred_element_type=F32)
    if nk == 1:
        finalize(part)
        return
    k = pl.program_id(2)

    @pl.when(k == 0)
    def _():
        acc_ref[...] = part

    @pl.when(jnp.logical_and(k > 0, k < nk - 1))
    def _():
        acc_ref[...] += part

    @pl.when(k == nk - 1)
    def _():
        finalize(acc_ref[...] + part)


def matmul(x, w, *, bias=None, res=None, act=None, out_dtype=F32, tm=1024, tn=512, tk=2048):
    m, kd = x.shape
    n = w.shape[1]
    tm, tn, tk = min(tm, m), min(tn, n), min(tk, kd)
    nk = kd // tk
    in_specs = [pl.BlockSpec((tm, tk), lambda i, j, k: (i, k)),
                pl.BlockSpec((tk, tn), lambda i, j, k: (k, j))]
    args = [x, w]
    if bias is not None:
        in_specs.append(pl.BlockSpec((1, tn), lambda i, j, k: (0, j)))
        args.append(bias.reshape(1, n).astype(F32))
    if res is not None:
        in_specs.append(pl.BlockSpec((tm, tn), lambda i, j, k: (i, j)))
        args.append(res)
    return pl.pallas_call(
        functools.partial(_mm_kernel, nk=nk, act=act, has_bias=bias is not None, has_res=res is not None),
        grid=(m // tm, n // tn, nk),
        in_specs=in_specs,
        out_specs=pl.BlockSpec((tm, tn), lambda i, j, k: (i, j)),
        out_shape=jax.ShapeDtypeStruct((m, n), out_dtype),
        scratch_shapes=[pltpu.VMEM((tm, tn), F32)] if nk > 1 else [],
        compiler_params=_cparams("parallel", "parallel", "arbitrary"), name="matmul",
    )(*args)


def _glu_kernel(x_ref, wv_ref, wg_ref, bv_ref, bg_ref, o_ref):
    x = x_ref[...]
    val = jnp.dot(x, wv_ref[...], preferred_element_type=F32) + bv_ref[...]
    gate = jnp.dot(x, wg_ref[...], preferred_element_type=F32) + bg_ref[...]
    o_ref[...] = val * jax.nn.sigmoid(gate)


def glu_matmul(x, w, b, *, tm=1024, tn=512):
    m, kd = x.shape
    n = w.shape[1] // 2
    nb = n // tn
    b2 = b.reshape(1, 2 * n).astype(F32)
    return pl.pallas_call(
        _glu_kernel, grid=(m // tm, nb),
        in_specs=[pl.BlockSpec((tm, kd), lambda i, j: (i, 0)),
                  pl.BlockSpec((kd, tn), lambda i, j: (0, j)),
                  pl.BlockSpec((kd, tn), lambda i, j: (0, j + nb)),
                  pl.BlockSpec((1, tn), lambda i, j: (0, j)),
                  pl.BlockSpec((1, tn), lambda i, j: (0, j + nb))],
        out_specs=pl.BlockSpec((tm, tn), lambda i, j: (i, j)),
        out_shape=jax.ShapeDtypeStruct((m, n), F32),
        compiler_params=_cparams("parallel", "parallel"), name="glu_matmul",
    )(x, w, w, b2, b2)


def _swiglu_part(x, wg_ref, wu_ref, wd_ref):
    g = jnp.dot(x, wg_ref[0], preferred_element_type=F32)
    u = jnp.dot(x, wu_ref[0], preferred_element_type=F32)
    a = (_silu(g) * u).astype(BF16)
    return jnp.dot(a, wd_ref[0], preferred_element_type=F32)


def _ffn_kernel(te_ref, tv_ref, x_ref, wg_ref, wu_ref, wd_ref, o_ref):
    f = pl.program_id(1)
    part = _swiglu_part(x_ref[...], wg_ref, wu_ref, wd_ref)

    @pl.when(f == 0)
    def _():
        o_ref[...] = part

    @pl.when(f > 0)
    def _():
        o_ref[...] += part


GATHER_ROWS_PER_STEP = 64


def _ffn_gather_kernel(te_ref, tv_ref, cur_ref, nxt_ref, h_hbm, wg_ref, wu_ref, wd_ref, o_ref,
                       xbuf_ref, xb_ref, acc_ref, sems, *, tm, nf, nt):
    i = pl.program_id(0)
    f = pl.program_id(1)
    slot = i % 2

    @pl.when(jnp.logical_and(i == 0, f == 0))
    def _():
        def start(r, carry):
            _token_row_dma(h_hbm, cur_ref[r], xbuf_ref.at[0], r, sems.at[0]).start()
            return carry
        lax.fori_loop(0, tm, start, 0, unroll=8)

    @pl.when(jnp.logical_and(i + 1 < nt, f < tm // GATHER_ROWS_PER_STEP))
    def _():
        for j in range(GATHER_ROWS_PER_STEP):
            r = f * GATHER_ROWS_PER_STEP + j
            _token_row_dma(h_hbm, nxt_ref[r], xbuf_ref.at[1 - slot], r, sems.at[1 - slot]).start()

    @pl.when(f == 0)
    def _():
        def wait(r, carry):
            _token_row_dma(h_hbm, cur_ref[r], xbuf_ref.at[slot], r, sems.at[slot]).wait()
            return carry
        lax.fori_loop(0, tm, wait, 0, unroll=8)

    @pl.when(jnp.logical_and(tv_ref[i] == 0, f == 0))
    def _():
        o_ref[...] = jnp.zeros_like(o_ref)

    @pl.when(tv_ref[i] != 0)
    def _():
        @pl.when(f == 0)
        def _():
            for c in range(ROW_CHUNKS):
                xb_ref[:, c * LANES:(c + 1) * LANES] = xbuf_ref[slot, _chunk_rows(c, tm), :].astype(BF16)

        part = _swiglu_part(xb_ref[...], wg_ref, wu_ref, wd_ref)

        @pl.when(f == 0)
        def _():
            acc_ref[...] = part

        @pl.when(jnp.logical_and(f > 0, f < nf - 1))
        def _():
            acc_ref[...] += part

        @pl.when(f == nf - 1)
        def _():
            for c in range(ROW_CHUNKS):
                cols = slice(c * LANES, (c + 1) * LANES)
                o_ref[_chunk_rows(c, tm), :] = acc_ref[:, cols] + part[:, cols]


def _ffn_weight_specs(d, tf, nf):
    def wcol(i, f, te, tv):
        return (te[i], 0, jnp.where(tv[i] != 0, f, nf - 1))

    def wrow(i, f, te, tv):
        return (te[i], jnp.where(tv[i] != 0, f, nf - 1), 0)

    return [pl.BlockSpec((1, d, tf), wcol), pl.BlockSpec((1, d, tf), wcol), pl.BlockSpec((1, tf, d), wrow)]


def grouped_ffn(x, w_gate, w_up, w_down, tile_expert, tile_valid, *, tm, tf=512):
    mp, d = x.shape
    nf = w_gate.shape[2] // tf
    row_block = pl.BlockSpec((tm, d), lambda i, f, te, tv: (i, 0))
    grid_spec = pltpu.PrefetchScalarGridSpec(
        num_scalar_prefetch=2, grid=(mp // tm, nf),
        in_specs=[row_block] + _ffn_weight_specs(d, tf, nf), out_specs=row_block)
    return pl.pallas_call(
        _ffn_kernel, grid_spec=grid_spec, out_shape=jax.ShapeDtypeStruct((mp, d), F32),
        compiler_params=_cparams("parallel", "arbitrary"), name="grouped_ffn",
    )(tile_expert, tile_valid, x, w_gate, w_up, w_down)


def gathered_ffn(h, row_token, w_gate, w_up, w_down, tile_expert, tile_valid, *, tm, tf=512):
    d = w_gate.shape[1]
    mp = row_token.shape[0]
    nt = mp // tm
    nf = w_gate.shape[2] // tf
    assert tm % GATHER_ROWS_PER_STEP == 0 and tm // GATHER_ROWS_PER_STEP <= nf
    smem_rows = functools.partial(pl.BlockSpec, (tm,), memory_space=pltpu.SMEM)
    grid_spec = pltpu.PrefetchScalarGridSpec(
        num_scalar_prefetch=2, grid=(nt, nf),
        in_specs=[smem_rows(lambda i, f, te, tv: (i,)),
                  smem_rows(lambda i, f, te, tv: (jnp.minimum(i + 1, nt - 1),)),
                  pl.BlockSpec(memory_space=pl.ANY)] + _ffn_weight_specs(d, tf, nf),
        out_specs=pl.BlockSpec((tm * ROW_CHUNKS, LANES), lambda i, f, te, tv: (i, 0)),
        scratch_shapes=[pltpu.VMEM((2, tm * ROW_CHUNKS, LANES), F32), pltpu.VMEM((tm, d), BF16),
                        pltpu.VMEM((tm, d), F32), pltpu.SemaphoreType.DMA((2,))])
    return pl.pallas_call(
        functools.partial(_ffn_gather_kernel, tm=tm, nf=nf, nt=nt), grid_spec=grid_spec,
        out_shape=jax.ShapeDtypeStruct((mp * ROW_CHUNKS, LANES), F32),
        compiler_params=_cparams("arbitrary", "arbitrary"), name="gathered_ffn",
    )(tile_expert, tile_valid, row_token, row_token, h, w_gate, w_up, w_down)


def _split_bf16(x):
    hi = x.astype(BF16)
    lo = (x - hi.astype(F32)).astype(BF16)
    return hi, lo


def _router_kernel(x_ref, g_ref, rhi_ref, rlo_ref, h_ref, mi_ref, mf_ref, cnt_ref, carry_ref, *, tm):
    i = pl.program_id(0)

    @pl.when(i == 0)
    def _():
        carry_ref[...] = jnp.zeros_like(carry_ref)

    h = _rms(x_ref[...], g_ref[...])
    for c in range(ROW_CHUNKS):
        h_ref[_chunk_rows(c, tm), :] = h[:, c * LANES:(c + 1) * LANES]
    h_hi, h_lo = _split_bf16(h)
    logits = (jnp.dot(h_hi, rhi_ref[...], preferred_element_type=F32)
              + (jnp.dot(h_lo, rhi_ref[...], preferred_element_type=F32)
                 + jnp.dot(h_hi, rlo_ref[...], preferred_element_type=F32)))
    lane = lax.broadcasted_iota(jnp.int32, (tm, LANES), 1)
    lg = jnp.where(lane < N_EXPERTS, logits, NEG_INF)
    m1 = jnp.max(lg, axis=1, keepdims=True)
    i1 = jnp.min(jnp.where(lg == m1, lane, LANES), axis=1, keepdims=True)
    lg2 = jnp.where(lane == i1, NEG_INF, lg)
    m2 = jnp.max(lg2, axis=1, keepdims=True)
    i2 = jnp.min(jnp.where(lg2 == m2, lane, LANES), axis=1, keepdims=True)
    e2 = jnp.exp(m2 - m1)
    w1 = 1.0 / (1.0 + e2)
    w2 = e2 / (1.0 + e2)
    oh1 = (lane == i1).astype(F32)
    oh2 = (lane == i2).astype(F32)
    tot = oh1 + oh2
    r = lax.broadcasted_iota(jnp.int32, (tm, tm), 0)
    c = lax.broadcasted_iota(jnp.int32, (tm, tm), 1)
    before = jnp.dot((c < r).astype(BF16), tot.astype(BF16), preferred_element_type=F32)
    base = before + carry_ref[0:1, :]
    rank1 = jnp.sum(oh1 * base, axis=1, keepdims=True).astype(jnp.int32)
    rank2 = jnp.sum(oh2 * base, axis=1, keepdims=True).astype(jnp.int32)
    mi_ref[...] = jnp.where(lane == 0, i1, jnp.where(lane == 1, i2, jnp.where(lane == 2, rank1, rank2)))
    mf_ref[...] = jnp.where(lane == 0, w1, w2)
    carry = carry_ref[...] + jnp.sum(tot, axis=0, keepdims=True)
    carry_ref[...] = carry
    cnt_ref[...] = carry


def route(x, g, router, *, tm=512):
    m, d = x.shape
    rpad = jnp.zeros((d, LANES), F32).at[:, :N_EXPERTS].set(router.astype(F32))
    rhi, rlo = _split_bf16(rpad)
    return pl.pallas_call(
        functools.partial(_router_kernel, tm=tm), grid=(m // tm,),
        in_specs=[_row_spec(tm, d), _vec_spec(d),
                  pl.BlockSpec((d, LANES), lambda i: (0, 0)), pl.BlockSpec((d, LANES), lambda i: (0, 0))],
        out_specs=[_row_spec(tm * ROW_CHUNKS, LANES), _row_spec(tm, LANES), _row_spec(tm, LANES),
                   pl.BlockSpec((8, LANES), lambda i: (0, 0))],
        out_shape=[jax.ShapeDtypeStruct((m * ROW_CHUNKS, LANES), F32), jax.ShapeDtypeStruct((m, LANES), jnp.int32),
                   jax.ShapeDtypeStruct((m, LANES), F32), jax.ShapeDtypeStruct((8, LANES), F32)],
        scratch_shapes=[pltpu.VMEM((8, LANES), F32)],
        compiler_params=_cparams("arbitrary"), name="moe_route",
    )(x, g.reshape(1, d), rhi, rlo)


def moe_layer(x, g, router, w_gate, w_up, w_down, *, expert_base=0, tm=512):
    n, d = x.shape
    h, meta_i, wts, counts = route(x, g, router)
    e1, e2, r1, r2 = (meta_i[:, k] for k in range(4))
    cnt = counts[0, :N_EXPERTS].astype(jnp.int32)
    padded = ((cnt + tm - 1) // tm) * tm
    ends = jnp.cumsum(padded)
    starts = ends - padded
    pos1 = starts[e1] + r1
    pos2 = starts[e2] + r2
    n_tiles = (2 * n) // tm + N_EXPERTS
    tile_start = jnp.arange(n_tiles, dtype=jnp.int32) * tm
    tile_expert = jnp.minimum(jnp.sum((tile_start[:, None] >= ends[None, :]).astype(jnp.int32), axis=1),
                              N_EXPERTS - 1)
    tile_valid = (tile_start < ends[-1]).astype(jnp.int32)
    tok = jnp.arange(n, dtype=jnp.int32)
    pos = jnp.concatenate([pos1, pos2])
    row_token = jnp.zeros((n_tiles * tm,), jnp.int32).at[pos].set(jnp.concatenate([tok, tok]))
    ys = gathered_ffn(h, row_token, w_gate, w_up, w_down, tile_expert + expert_base, tile_valid, tm=tm)
    return ys, pos, wts


def _gdn_gates_kernel(ba_ref, alog_ref, dt_ref, o_ref, *, tm):
    ba = ba_ref[...]
    sp = ba + dt_ref[...]
    softplus = jnp.maximum(sp, 0.0) + jnp.log1p(jnp.exp(-jnp.abs(sp)))
    gcum = -jnp.exp(alog_ref[...]) * softplus
    inchunk = lax.broadcasted_iota(jnp.int32, (tm, LANES), 0) % GDN_CHUNK
    sh = 1
    while sh < GDN_CHUNK:
        gcum = gcum + jnp.where(inchunk >= sh, pltpu.roll(gcum, sh, 0), 0.0)
        sh *= 2
    lane = lax.broadcasted_iota(jnp.int32, (tm, LANES), 1)
    o_ref[...] = jnp.where(lane < GDN_V_HEADS, jax.nn.sigmoid(ba), gcum)


def gdn_gates(ba, a_log, dt_bias, *, tm=512):
    m = ba.shape[0]
    alog_row = jnp.zeros((1, LANES), F32).at[0, GDN_V_HEADS:2 * GDN_V_HEADS].set(a_log.astype(F32))
    dt_row = jnp.zeros((1, LANES), F32).at[0, GDN_V_HEADS:2 * GDN_V_HEADS].set(dt_bias.astype(F32))
    return pl.pallas_call(
        functools.partial(_gdn_gates_kernel, tm=tm), grid=(m // tm,),
        in_specs=[_row_spec(tm, LANES), _vec_spec(LANES), _vec_spec(LANES)],
        out_specs=_row_spec(tm, LANES), out_shape=jax.ShapeDtypeStruct((m, LANES), F32),
        compiler_params=_cparams("parallel"), name="gdn_gates",
    )(ba, alog_row, dt_row)


def _gdn_kernel(q_ref, k_ref, v_ref, z_ref, gates_ref, cwq_ref, cwk_ref, cwv_ref, nw_ref,
                o_ref, state_ref, hq_ref, hk_ref, hv_ref, xq_ref, xk_ref, xv_ref, *, ts, n_pairs):
    first_head = pl.program_id(1) * (2 * n_pairs)
    s = pl.program_id(2)
    hd = HEAD_DIM

    @pl.when(s == 0)
    def _():
        state_ref[...] = jnp.zeros_like(state_ref)
        hq_ref[...] = jnp.zeros_like(hq_ref)
        hk_ref[...] = jnp.zeros_like(hk_ref)
        hv_ref[...] = jnp.zeros_like(hv_ref)

    def conv_silu(x_ref, hist_ref, xx_ref, w_ref):
        xx_ref[0:8, :] = hist_ref[...]
        xx_ref[8:, :] = x_ref[...]
        hist_ref[...] = x_ref[ts - 8:, :]
        w = w_ref[...]
        acc = w[0:1, :] * xx_ref[5:5 + ts, :]
        for j in range(1, GDN_CONV):
            acc = acc + w[j:j + 1, :] * xx_ref[5 + j:5 + j + ts, :]
        return _silu(acc)

    def l2n(x):
        return x * lax.rsqrt(jnp.sum(x * x, axis=-1, keepdims=True) + EPS)

    q_all = conv_silu(q_ref, hq_ref, xq_ref, cwq_ref)
    k_all = conv_silu(k_ref, hk_ref, xk_ref, cwk_ref)
    v = conv_silu(v_ref, hv_ref, xv_ref, cwv_ref)
    q = [l2n(q_all[:, g * hd:(g + 1) * hd]) * (hd ** -0.5) for g in range(n_pairs)]
    k = [l2n(k_all[:, g * hd:(g + 1) * hd]) for g in range(n_pairs)]
    heads = range(2 * n_pairs)

    gates = gates_ref[...]
    lane = lax.broadcasted_iota(jnp.int32, (ts, LANES), 1)

    def pick(idx):
        return jnp.sum(jnp.where(lane == idx, gates, 0.0), axis=1, keepdims=True)

    beta = [pick(first_head + h) for h in heads]
    gc = [pick(GDN_V_HEADS + first_head + h) for h in heads]
    egc = [jnp.exp(x) for x in gc]

    sc = 2 * GDN_CHUNK
    r128 = lax.broadcasted_iota(jnp.int32, (sc, sc), 0)
    c128 = lax.broadcasted_iota(jnp.int32, (sc, sc), 1)
    same_chunk = (r128 // GDN_CHUNK) == (c128 // GDN_CHUNK)
    incl = jnp.logical_and(same_chunk, c128 <= r128)
    strict = jnp.logical_and(same_chunk, c128 < r128)

    def mm(a, b):
        return jnp.dot(a.astype(BF16), b.astype(BF16), preferred_element_type=F32)

    qb = [x.astype(BF16) for x in q]
    kb = [x.astype(BF16) for x in k]
    n_blk = ts // sc
    chains = [(u, h) for u in range(n_blk) for h in heads]
    blk_rows = [slice(u * sc, (u + 1) * sc) for u in range(n_blk)]
    ab = [[lax.dot_general(jnp.concatenate([qb[g][r], kb[g][r]], axis=0), kb[g][r], (((1,), (1,)), ((), ())),
                           preferred_element_type=F32) for r in blk_rows] for g in range(n_pairs)]
    kt = [[k[g][r].T.astype(BF16) for r in blk_rows] for g in range(n_pairs)]
    eye = (r128 == c128).astype(F32)
    same16 = (r128 // 16) == (c128 // 16)
    same32 = (r128 // 32) == (c128 // 32)
    lmat, attn, ymat = [], [], []
    for u, h in chains:
        r = blk_rows[u]
        g = h // 2
        gcol, bcol, ecol = gc[h][r], beta[h][r], egc[h][r]
        grow = jnp.broadcast_to(gcol, (sc, sc)).T
        decay = jnp.exp(jnp.where(incl, gcol - grow, NEG_INF))
        lmat.append(jnp.where(strict, ab[g][u][sc:] * bcol * decay, 0.0))
        attn.append(ab[g][u][:sc] * decay)
        ymat.append(jnp.concatenate([bcol * v[r, h * hd:(h + 1) * hd], (bcol * ecol) * k[g][r]], axis=1))
    nd = [jnp.where(same16, -m, 0.0) for m in lmat]
    x = [eye + m for m in nd]
    pw = [mm(m, m) for m in nd]
    for level in range(3):
        xp = [mm(a, b) for a, b in zip(x, pw)]
        if level < 2:
            pw = [mm(m, m) for m in pw]
        x = [a + b for a, b in zip(x, xp)]
    for inner, outer in ((same16, same32), (same32, same_chunk)):
        off = [jnp.where(jnp.logical_and(outer, jnp.logical_not(inner)), m, 0.0) for m in lmat]
        t1 = [mm(a, b) for a, b in zip(x, off)]
        t2 = [mm(a, b) for a, b in zip(t1, x)]
        x = [a - b for a, b in zip(x, t2)]
    uw = [mm(a, b) for a, b in zip(x, ymat)]
    dw = [mm(a, b) for a, b in zip(attn, uw)]
    steps = [[] for _ in heads]
    for i, (u, h) in enumerate(chains):
        r = blk_rows[u]
        cmat = egc[h][r] * q[h // 2][r] - dw[i][:, hd:]
        for c in range(2):
            lrow = slice(c * GDN_CHUNK, (c + 1) * GDN_CHUNK)
            last = u * sc + (c + 1) * GDN_CHUNK - 1
            gl = gc[h][last:last + 1]
            tail = jnp.exp(gl - gc[h][r][lrow]) * uw[i][lrow]
            zeros = jnp.zeros_like(tail)
            full = jnp.concatenate([tail, zeros] if c == 0 else [zeros, tail], axis=0)
            bm = jnp.dot(kt[h // 2][u], full.astype(BF16), preferred_element_type=F32)
            mc = jnp.concatenate([bm[:, hd:], cmat[lrow]], axis=0).astype(BF16)
            steps[h].append((mc, bm[:, :hd], dw[i][lrow, :hd], jnp.exp(gl)))

    states = [state_ref[h] for h in heads]
    outs = [[] for _ in heads]
    for cc in range(ts // GDN_CHUNK):
        for h in heads:
            mc, bmat, dmat, glast = steps[h][cc]
            ms = jnp.dot(mc, states[h].astype(BF16), preferred_element_type=F32)
            outs[h].append(ms[hd:] + dmat)
            states[h] = glast * states[h] + (bmat - ms[:hd])
    for h in heads:
        state_ref[h] = states[h]

    z = z_ref[...]
    nw = nw_ref[...]
    ys = []
    for h in heads:
        o = jnp.concatenate(outs[h], axis=0)
        ys.append(_rms(o, nw) * _silu(z[:, h * hd:(h + 1) * hd]))
    o_ref[...] = jnp.concatenate(ys, axis=1).astype(o_ref.dtype)


def gdn_core(proj, gates, conv_w, norm_w, *, batch, seq, ts=256, n_pairs=4):
    hd = HEAD_DIM
    ns = seq // ts
    qw, vw = n_pairs * hd, 2 * n_pairs * hd
    kq = GDN_QK_DIM // qw
    vb = (2 * GDN_QK_DIM) // vw
    zb = GDN_CONV_DIM // vw

    def rows(b, p, s):
        return b * ns + s

    in_specs = [
        pl.BlockSpec((ts, qw), lambda b, p, s: (rows(b, p, s), p)),
        pl.BlockSpec((ts, qw), lambda b, p, s: (rows(b, p, s), kq + p)),
        pl.BlockSpec((ts, vw), lambda b, p, s: (rows(b, p, s), vb + p)),
        pl.BlockSpec((ts, vw), lambda b, p, s: (rows(b, p, s), zb + p)),
        pl.BlockSpec((ts, LANES), lambda b, p, s: (rows(b, p, s), 0)),
        pl.BlockSpec((GDN_CONV, qw), lambda b, p, s: (0, p)),
        pl.BlockSpec((GDN_CONV, qw), lambda b, p, s: (0, kq + p)),
        pl.BlockSpec((GDN_CONV, vw), lambda b, p, s: (0, vb + p)),
        pl.BlockSpec((1, hd), lambda b, p, s: (0, 0)),
    ]
    return pl.pallas_call(
        functools.partial(_gdn_kernel, ts=ts, n_pairs=n_pairs), grid=(batch, GDN_QK_HEADS // n_pairs, ns),
        in_specs=in_specs,
        out_specs=pl.BlockSpec((ts, vw), lambda b, p, s: (rows(b, p, s), p)),
        out_shape=jax.ShapeDtypeStruct((batch * seq, GDN_V_DIM), BF16),
        scratch_shapes=[pltpu.VMEM((2 * n_pairs, hd, hd), F32),
                        pltpu.VMEM((8, qw), F32), pltpu.VMEM((8, qw), F32), pltpu.VMEM((8, vw), F32),
                        pltpu.VMEM((ts + 8, qw), F32), pltpu.VMEM((ts + 8, qw), F32),
                        pltpu.VMEM((ts + 8, vw), F32)],
        compiler_params=_cparams("parallel", "parallel", "arbitrary"), name="gdn_core",
    )(proj, proj, proj, proj, gates, conv_w, conv_w, conv_w, norm_w.reshape(1, hd).astype(F32))


def gated_deltanet(x, h, w_in, conv_w, a_log, dt_bias, norm_w, w_out, *, batch, seq):
    main = GDN_CONV_DIM + GDN_V_DIM
    w_main = w_in[:, :main].astype(BF16)
    w_ba = jnp.zeros((w_in.shape[0], LANES), F32).at[:, :2 * GDN_V_HEADS].set(w_in[:, main:]).astype(BF16)
    proj = matmul(h, w_main)
    gates = gdn_gates(matmul(h, w_ba), a_log, dt_bias)
    y = gdn_core(proj, gates, conv_w.astype(F32), norm_w, batch=batch, seq=seq)
    return matmul(y, w_out.astype(BF16), res=x)


def _sgu_kernel(u_ref, v_ref, g_ref, b_ref, ws_ref, bs_ref, o_ref, *, nch):
    r = lax.broadcasted_iota(jnp.int32, (SGU_CHUNK, SGU_CHUNK), 0)
    c = lax.broadcasted_iota(jnp.int32, (SGU_CHUNK, SGU_CHUNK), 1)
    causal = c <= r
    gd = SGU_GROUP_DIM
    for ch in range(nch):
        rows = slice(ch * SGU_CHUNK, (ch + 1) * SGU_CHUNK)
        v = v_ref[rows, :]
        mu = jnp.mean(v, axis=-1, keepdims=True)
        vc = v - mu
        vn = vc * lax.rsqrt(jnp.mean(vc * vc, axis=-1, keepdims=True) + LN_EPS)
        vn = (vn * g_ref[...] + b_ref[...]).astype(BF16)
        for grp in range(SGU_GROUPS):
            cols = slice(grp * gd, (grp + 1) * gd)
            w = jnp.where(causal, ws_ref[grp], 0.0).astype(BF16)
            sv = jnp.dot(w, vn[:, cols], preferred_element_type=F32) + bs_ref[:, grp:grp + 1]
            o_ref[rows, cols] = (u_ref[rows, cols] * sv).astype(o_ref.dtype)


def sgu_spatial(p, ln_g, ln_b, w_s, b_s, *, nch=2):
    m = p.shape[0]
    rows = nch * SGU_CHUNK
    inner = SGU_INNER
    return pl.pallas_call(
        functools.partial(_sgu_kernel, nch=nch), grid=(m // rows,),
        in_specs=[pl.BlockSpec((rows, inner), lambda i: (i, 0)),
                  pl.BlockSpec((rows, inner), lambda i: (i, 1)),
                  _vec_spec(inner), _vec_spec(inner),
                  pl.BlockSpec((SGU_GROUPS, SGU_CHUNK, SGU_CHUNK), lambda i: (0, 0, 0)),
                  pl.BlockSpec((SGU_CHUNK, SGU_GROUPS), lambda i: (0, 0))],
        out_specs=pl.BlockSpec((rows, inner), lambda i: (i, 0)),
        out_shape=jax.ShapeDtypeStruct((m, inner), BF16),
        compiler_params=_cparams("parallel"), name="sgu_spatial",
    )(p, p, ln_g.reshape(1, inner).astype(F32), ln_b.reshape(1, inner).astype(F32),
      w_s.astype(F32), b_s.T.astype(F32))


def chunked_sgu(x, h, w_in, b_in, ln_g, ln_b, w_s, b_s, w_out, b_out):
    p = matmul(h, w_in.astype(BF16), bias=b_in, act="gelu")
    uv = sgu_spatial(p, ln_g, ln_b, w_s, b_s)
    return matmul(uv, w_out.astype(BF16), bias=b_out, res=x)


CV_HALO = 32


def _cvconv_kernel(halo_ref, y_ref, w_ref, bdw_ref, g_ref, b_ref, o_ref, xx_ref, yc_ref, *, ts, cb):
    s = pl.program_id(1)
    d = y_ref.shape[1]

    @pl.when(s == 0)
    def _():
        xx_ref[0:CV_HALO, :] = jnp.zeros((CV_HALO, d), F32)

    @pl.when(s > 0)
    def _():
        xx_ref[0:CV_HALO, :] = halo_ref[...]

    xx_ref[CV_HALO:, :] = y_ref[...]
    off = CV_HALO - (CONV_WIDTH - 1)
    for blk in range(d // cb):
        cols = slice(blk * cb, (blk + 1) * cb)
        w = w_ref[:, cols]
        acc = w[0:1, :] * xx_ref[off:off + ts, cols]
        for j in range(1, CONV_WIDTH):
            acc = acc + w[j:j + 1, :] * xx_ref[off + j:off + j + ts, cols]
        yc_ref[:, cols] = acc + bdw_ref[:, cols]
    y = yc_ref[...]
    mu = jnp.mean(y, axis=-1, keepdims=True)
    yc = y - mu
    yn = yc * lax.rsqrt(jnp.mean(yc * yc, axis=-1, keepdims=True) + LN_EPS)
    o_ref[...] = _silu(yn * g_ref[...] + b_ref[...]).astype(o_ref.dtype)


def conformer_dwconv(y, w_dw, b_dw, ln_g, ln_b, *, batch, seq, ts=128, cb=256):
    m, d = y.shape
    ns = seq // ts
    hb = ts // CV_HALO
    return pl.pallas_call(
        functools.partial(_cvconv_kernel, ts=ts, cb=cb), grid=(batch, ns),
        in_specs=[pl.BlockSpec((CV_HALO, d), lambda b, s: (jnp.maximum((b * ns + s) * hb - 1, 0), 0)),
                  pl.BlockSpec((ts, d), lambda b, s: (b * ns + s, 0)),
                  pl.BlockSpec((CONV_WIDTH, d), lambda b, s: (0, 0)),
                  pl.BlockSpec((1, d), lambda b, s: (0, 0)),
                  pl.BlockSpec((1, d), lambda b, s: (0, 0)),
                  pl.BlockSpec((1, d), lambda b, s: (0, 0))],
        out_specs=pl.BlockSpec((ts, d), lambda b, s: (b * ns + s, 0)),
        out_shape=jax.ShapeDtypeStruct((m, d), BF16),
        scratch_shapes=[pltpu.VMEM((ts + CV_HALO, d), F32), pltpu.VMEM((ts, d), F32)],
        compiler_params=_cparams("parallel", "parallel"), name="conformer_dwconv",
    )(y, y, w_dw.astype(F32), b_dw.reshape(1, d).astype(F32), ln_g.reshape(1, d).astype(F32),
      ln_b.reshape(1, d).astype(F32))


def conformer_conv(x, h, w_pw1, b_pw1, w_dw, b_dw, ln_g, ln_b, w_pw2, b_pw2, *, batch, seq):
    y = glu_matmul(h, w_pw1.astype(BF16), b_pw1)
    yn = conformer_dwconv(y, w_dw, b_dw, ln_g, ln_b, batch=batch, seq=seq)
    return matmul(yn, w_pw2.astype(BF16), bias=b_pw2, res=x)


def dense_ffn(h, w_gate, w_up, w_down, *, tm=1024):
    m = h.shape[0]
    nt = m // tm
    return grouped_ffn(h, w_gate[None].astype(BF16), w_up[None].astype(BF16), w_down[None].astype(BF16),
                       jnp.zeros((nt,), jnp.int32), jnp.ones((nt,), jnp.int32), tm=tm)


def kernel(x, mix_norm, ffn_norm, gdn_w_in, gdn_conv_w, gdn_a_log, gdn_dt_bias, gdn_norm, gdn_w_out, sgu_w_in, sgu_b_in, sgu_ln_g, sgu_ln_b, sgu_w_s, sgu_b_s, sgu_w_out, sgu_b_out, cv_w_pw1, cv_b_pw1, cv_w_dw, cv_b_dw, cv_ln_g, cv_ln_b, cv_w_pw2, cv_b_pw2, ffn_w_gate, ffn_w_up, ffn_w_down, moe_router, moe_w_gate, moe_w_up, moe_w_down, final_norm):
    batch, seq, d = x.shape
    depth = mix_norm.shape[0]
    xr = x.reshape(batch * seq, d).astype(F32)
    pending = None
    moe_wg, moe_wu, moe_wd = (w.astype(BF16).reshape((-1,) + w.shape[2:]) for w in (moe_w_gate, moe_w_up, moe_w_down))
    for i in range(depth):
        if pending is None:
            h = rmsnorm(xr, mix_norm[i])
        elif pending[0] == "dense":
            xr, h = add_rmsnorm(xr, pending[1], mix_norm[i])
        else:
            xr, h = combine_rmsnorm(xr, *pending[1:], mix_norm[i], final=False)
        kind, j = i % N_MIXERS, i // N_MIXERS
        if kind == 0:
            xr = gated_deltanet(xr, h, gdn_w_in[j], gdn_conv_w[j], gdn_a_log[j], gdn_dt_bias[j], gdn_norm[j],
                                gdn_w_out[j], batch=batch, seq=seq)
        elif kind == 1:
            xr = chunked_sgu(xr, h, sgu_w_in[j], sgu_b_in[j], sgu_ln_g[j], sgu_ln_b[j], sgu_w_s[j], sgu_b_s[j],
                             sgu_w_out[j], sgu_b_out[j])
        else:
            xr = conformer_conv(xr, h, cv_w_pw1[j], cv_b_pw1[j], cv_w_dw[j], cv_b_dw[j], cv_ln_g[j], cv_ln_b[j],
                                cv_w_pw2[j], cv_b_pw2[j], batch=batch, seq=seq)
        j = i // 2
        if i % 2 == 0:
            h = rmsnorm(xr, ffn_norm[i])
            pending = ("dense", dense_ffn(h, ffn_w_gate[j], ffn_w_up[j], ffn_w_down[j]))
        else:
            pending = ("moe",) + moe_layer(xr, ffn_norm[i], moe_router[j], moe_wg, moe_wu, moe_wd,
                                           expert_base=j * N_EXPERTS)
    if pending[0] == "dense":
        xr, _ = add_rmsnorm(xr, pending[1], final_norm)
        out = _final_rms(xr, final_norm)
    else:
        out = combine_rmsnorm(xr, *pending[1:], final_norm, final=True)
    return out.reshape(batch, seq, d).astype(x.dtype)


def _final_rms(x, g, *, tm=512):
    m, d = x.shape
    return pl.pallas_call(
        _rms_kernel, grid=(m // tm,),
        in_specs=[_row_spec(tm, d), _vec_spec(d)],
        out_specs=_row_spec(tm, d),
        out_shape=jax.ShapeDtypeStruct((m, d), F32),
        compiler_params=_cparams("parallel"), name="final_rmsnorm",
    )(x, g.reshape(1, d))
```
